```python
import jax, jax.numpy as jnp
from jax import lax
import numpy as np

D_MODEL = 2048
BATCH = 4
SEQ = 4096
DEPTH = 2

GRID_W = 64
CTX_LEN = 256
N_MIXERS = 2
EPS = 1e-6
RWKV_HEAD = 64
RWKV_HEADS = D_MODEL // RWKV_HEAD
RWKV_LORA = max(32, int(round(1.8 * D_MODEL ** 0.5 / 32)) * 32)
RWKV_GN_EPS = 64e-5
N_SHIFT_TARGETS = 6
RET_HEADS = 8
RET_QK_HEAD = D_MODEL // RET_HEADS
RET_V_DIM = 2 * D_MODEL
RET_V_HEAD = RET_V_DIM // RET_HEADS
RET_IN_DIM = 2 * D_MODEL + 2 * RET_V_DIM
RET_CHUNK = 64
ROPE_BASE = 10000.0
N_RWKV_LAYERS = (DEPTH + N_MIXERS - 1) // N_MIXERS
N_RET_LAYERS = DEPTH // N_MIXERS

kernel_name = "hybrid_rwkv7_retention_dit_trunk"

F32 = jnp.float32


def rmsnorm(x, g):
    xf = x.astype(F32)
    y = xf * lax.rsqrt(jnp.mean(xf * xf, axis=-1, keepdims=True) + EPS)
    return (y * g.astype(F32)).astype(x.dtype)


def adaln(cvec, w, b):
    m = jax.nn.silu(cvec) @ w + b
    return jnp.split(m[:, None, :], 3, axis=-1)


def qshift_grid(x):
    B, L, D = x.shape
    rows = L // GRID_W
    g = x.reshape(B, rows, GRID_W, D)
    q = D // 4
    left = jnp.pad(g[:, :, :-1, :q], ((0, 0), (0, 0), (1, 0), (0, 0)))
    right = jnp.pad(g[:, :, 1:, q:2 * q], ((0, 0), (0, 0), (0, 1), (0, 0)))
    up = jnp.pad(g[:, :-1, :, 2 * q:3 * q], ((0, 0), (1, 0), (0, 0), (0, 0)))
    down = jnp.pad(g[:, 1:, :, 3 * q:], ((0, 0), (0, 1), (0, 0), (0, 0)))
    return jnp.concatenate([left, right, up, down], axis=-1).reshape(B, L, D)


def shift_seq(x):
    h = x.shape[-1] // 2
    prev = jnp.pad(x[:, :-1, :h], ((0, 0), (1, 0), (0, 0)))
    nxt = jnp.pad(x[:, 1:, h:], ((0, 0), (0, 1), (0, 0)))
    return jnp.concatenate([prev, nxt], axis=-1)


def rope_2d(x):
    L, d = x.shape[1], x.shape[-1]
    t = jnp.arange(L)
    row = (t // GRID_W).astype(F32)
    col = (t % GRID_W).astype(F32)
    nf = d // 4
    inv = ROPE_BASE ** (-jnp.arange(nf, dtype=F32) / nf)
    ang = jnp.concatenate([row[:, None] * inv, col[:, None] * inv], axis=-1)
    cos = jnp.cos(ang)[None, :, None, :]
    sin = jnp.sin(ang)[None, :, None, :]
    xf = x.astype(F32)
    x1, x2 = xf[..., :d // 2], xf[..., d // 2:]
    return jnp.concatenate([x1 * cos - x2 * sin, x1 * sin + x2 * cos], axis=-1).astype(x.dtype)


def rwkv_heads(t):
    return t.reshape(t.shape[:-1] + (RWKV_HEADS, RWKV_HEAD)).astype(F32)


def rwkv7_project(h, h_shift, mix, w_in, w0, w1, w2, a0, a1, a2, k_k, k_a):
    xm = h[:, :, None, :] + (h_shift - h)[:, :, None, :] * mix
    rkvg = jnp.einsum('bljd,jde->blje', xm[:, :, :4], w_in)
    r, k, v, g = rkvg[:, :, 0], rkvg[:, :, 1], rkvg[:, :, 2], rkvg[:, :, 3]
    xw, xa = xm[:, :, 4], xm[:, :, 5]
    ww = w0[:, None, None, :] + jnp.einsum('nblr,nre->nble', jnp.tanh(jnp.einsum('bld,ndr->nblr', xw, w1)), w2)
    decay = jnp.exp(-jnp.exp(-jax.nn.softplus(-ww.astype(F32)) - 0.5))
    a = jax.nn.sigmoid((a0[:, None, None, :] + jnp.einsum('nblr,nre->nble', jnp.einsum('bld,ndr->nblr', xa, a1), a2)).astype(F32))
    kk = rwkv_heads(k * k_k)
    kk = kk / jnp.maximum(jnp.sqrt(jnp.sum(kk * kk, axis=-1, keepdims=True)), 1e-12)
    a_h = rwkv_heads(a)
    k_dirs = rwkv_heads(k)[None] * (1.0 + (a_h - 1.0) * k_a.reshape(RWKV_HEADS, RWKV_HEAD).astype(F32))
    return rwkv_heads(r), k_dirs, rwkv_heads(v), kk, rwkv_heads(decay), a_h, g


def wkv7_scan(S0, r, decay, k, v, kk, a, reverse):
    def step(S, inp):
        r_t, w_t, k_t, v_t, kk_t, a_t = inp
        sa = jnp.einsum('bhvk,bhk->bhv', S, kk_t)
        S = S * w_t[:, :, None, :] - sa[..., None] * (kk_t * a_t)[:, :, None, :] + v_t[..., None] * k_t[:, :, None, :]
        return S, jnp.einsum('bhvk,bhk->bhv', S, r_t)
    xs = tuple(jnp.swapaxes(t, 0, 1) for t in (r, decay, k, v, kk, a))
    S, o = lax.scan(step, S0, xs, reverse=reverse)
    return S, jnp.swapaxes(o, 0, 1)


def rwkv7_output(o, r, k_dirs, v, g, r_k, ln_g, ln_b, w_out, dtype):
    B, L = o.shape[:2]
    mu = jnp.mean(o, axis=-1, keepdims=True)
    var = jnp.mean(jnp.square(o - mu), axis=-1, keepdims=True)
    on = ((o - mu) * lax.rsqrt(var + RWKV_GN_EPS)).reshape(B, L, D_MODEL) * ln_g + ln_b
    bonus = (jnp.sum(r * k_dirs.sum(0) * r_k.astype(F32), axis=-1, keepdims=True) * v).reshape(B, L, D_MODEL)
    y = (on + bonus) * jax.nn.silu(g.astype(F32))
    return y.astype(dtype) @ w_out


def rwkv7_mixer(h_lat, h_ctx, need_ctx, mix, w_in, w0, w1, w2, a0, a1, a2, k_k, k_a, r_k, ln_g, ln_b, w_out):
    prm = (mix, w_in, w0, w1, w2, a0, a1, a2, k_k, k_a)
    r_l, k_l, v_l, kk_l, w_l, a_l, g_l = rwkv7_project(h_lat, qshift_grid(h_lat), *prm)
    r_c, k_c, v_c, kk_c, w_c, a_c, g_c = rwkv7_project(h_ctx, shift_seq(h_ctx), *prm)
    S0 = jnp.zeros((h_lat.shape[0], RWKV_HEADS, RWKV_HEAD, RWKV_HEAD), F32)
    o_lat, o_ctx = 0.0, 0.0
    for d in range(2):
        S_c, oc = wkv7_scan(S0, r_c, w_c[d], k_c[d], v_c, kk_c, a_c[d], d == 1)
        _, ol = wkv7_scan(S_c, r_l, w_l[d], k_l[d], v_l, kk_l, a_l[d], d == 1)
        o_lat = o_lat + ol
        o_ctx = o_ctx + oc
    y_lat = rwkv7_output(o_lat, r_l, k_l, v_l, g_l, r_k, ln_g, ln_b, w_out, h_lat.dtype)
    y_ctx = rwkv7_output(o_ctx, r_c, k_c, v_c, g_c, r_k, ln_g, ln_b, w_out, h_ctx.dtype) if need_ctx else None
    return y_lat, y_ctx


def retention_project(h, w_in, rotate):
    B, L, _ = h.shape
    qkvg = h @ w_in
    q = qkvg[..., :D_MODEL].reshape(B, L, RET_HEADS, RET_QK_HEAD)
    k = qkvg[..., D_MODEL:2 * D_MODEL].reshape(B, L, RET_HEADS, RET_QK_HEAD)
    v = qkvg[..., 2 * D_MODEL:2 * D_MODEL + RET_V_DIM].reshape(B, L, RET_HEADS, RET_V_HEAD)
    g = qkvg[..., 2 * D_MODEL + RET_V_DIM:]
    if rotate:
        q, k = rope_2d(q), rope_2d(k)
    return q.astype(F32), k.astype(F32) * (RET_QK_HEAD ** -0.5), v.astype(F32), g


def retention_scan(R0, q, k, v, lg, reverse):
    B, L, H, dk = q.shape
    dv = v.shape[-1]
    C = RET_CHUNK
    nC = L // C
    p = jnp.arange(C, dtype=F32)
    if reverse:
        p = C - 1 - p
    diff = p[:, None] - p[None, :]
    mask = (diff > 0) if reverse else (diff >= 0)
    Dm = jnp.where(mask[None], jnp.exp(jnp.where(mask, diff, 0.0)[None] * lg[:, None, None]), 0.0)
    dq = jnp.exp((p + 1)[:, None] * lg[None, :])[None, :, :, None]
    dkey = jnp.exp((C - 1 - p)[:, None] * lg[None, :])[None, :, :, None]
    dchunk = jnp.exp(C * lg)[None, :, None, None]

    def step(R, inp):
        qc, kc, vc = inp
        s = jnp.einsum('bnhd,bmhd->bhnm', qc, kc) * Dm[None]
        o = jnp.einsum('bhnm,bmhe->bnhe', s, vc) + jnp.einsum('bnhd,bhde->bnhe', qc, R) * dq
        R = R * dchunk + jnp.einsum('bmhd,bmhe->bhde', kc * dkey, vc)
        return R, o

    xs = tuple(jnp.swapaxes(t.reshape(B, nC, C, H, t.shape[-1]), 0, 1) for t in (q, k, v))
    R, o = lax.scan(step, R0, xs, reverse=reverse)
    return R, jnp.swapaxes(o, 0, 1).reshape(B, L, H, dv)


def retention_output(o, g, gn_g, w_out, dtype):
    B, L = o.shape[:2]
    on = o * lax.rsqrt(jnp.mean(o * o, axis=-1, keepdims=True) + EPS)
    y = on.reshape(B, L, RET_V_DIM) * gn_g * jax.nn.silu(g.astype(F32))
    return y.astype(dtype) @ w_out


def retention_mixer(h_lat, h_ctx, need_ctx, w_in, decay_logit, gn_g, w_out):
    q_l, k_l, v_l, g_l = retention_project(h_lat, w_in, True)
    q_c, k_c, v_c, g_c = retention_project(h_ctx, w_in, False)
    lg = jax.nn.log_sigmoid(decay_logit.astype(F32))
    R0 = jnp.zeros((h_lat.shape[0], RET_HEADS, RET_QK_HEAD, RET_V_HEAD), F32)
    o_lat, o_ctx = 0.0, 0.0
    for d in range(2):
        R_c, oc = retention_scan(R0, q_c, k_c, v_c, lg[d], d == 1)
        _, ol = retention_scan(R_c, q_l, k_l, v_l, lg[d], d == 1)
        o_lat = o_lat + ol
        o_ctx = o_ctx + oc
    y_lat = retention_output(o_lat, g_l, gn_g, w_out, h_lat.dtype)
    y_ctx = retention_output(o_ctx, g_c, gn_g, w_out, h_ctx.dtype) if need_ctx else None
    return y_lat, y_ctx


def setup_inputs(seed: int = 0) -> dict:
    key = jax.random.key(seed)
    ks = iter(jax.random.split(key, 40))

    def nrm(shape, std):
        return jax.random.normal(next(ks), shape, F32) * std

    D = D_MODEL
    NR, NT = N_RWKV_LAYERS, N_RET_LAYERS
    w0_base = jnp.linspace(-6.5, -1.5, D, dtype=F32)
    ret_logit = jnp.log(2.0 ** (5.0 + jnp.arange(RET_HEADS, dtype=F32)) - 1.0)
    return {
        "x": nrm((BATCH, SEQ, D), 1.0),
        "c": nrm((BATCH, D), 1.0),
        "ctx": nrm((BATCH, CTX_LEN, D), 1.0),
        "c_ctx": nrm((D,), 1.0),
        "ada_w": nrm((DEPTH, D, 3 * D), 0.5 * D ** -0.5),
        "ada_b": nrm((DEPTH, 3 * D), 0.02),
        "norm_g": 1.0 + nrm((DEPTH, D), 0.02),
        "rk_mix": jax.random.uniform(next(ks), (NR, N_SHIFT_TARGETS, D), F32),
        "rk_w_in": nrm((NR, 4, D, D), D ** -0.5),
        "rk_w0": w0_base + nrm((NR, 2, D), 0.1),
        "rk_w1": nrm((NR, 2, D, RWKV_LORA), D ** -0.5),
        "rk_w2": nrm((NR, 2, RWKV_LORA, D), 0.1 * RWKV_LORA ** -0.5),
        "rk_a0": nrm((NR, 2, D), 0.1),
        "rk_a1": nrm((NR, 2, D, RWKV_LORA), D ** -0.5),
        "rk_a2": nrm((NR, 2, RWKV_LORA, D), 0.1 * RWKV_LORA ** -0.5),
        "rk_k_k": 0.85 + nrm((NR, D), 0.02),
        "rk_k_a": 1.0 + nrm((NR, D), 0.02),
        "rk_r_k": nrm((NR, RWKV_HEADS, RWKV_HEAD), 0.1),
        "rk_ln_g": 1.0 + nrm((NR, D), 0.02),
        "rk_ln_b": nrm((NR, D), 0.02),
        "rk_w_out": nrm((NR, D, D), D ** -0.5),
        "rt_w_in": nrm((NT, D, RET_IN_DIM), D ** -0.5),
        "rt_decay_logit": ret_logit + nrm((NT, 2, RET_HEADS), 0.05),
        "rt_gn_g": 1.0 + nrm((NT, RET_V_DIM), 0.02),
        "rt_w_out": nrm((NT, RET_V_DIM, D), RET_V_DIM ** -0.5),
        "final_g": 1.0 + nrm((D,), 0.02),
    }


def reference(x, c, ctx, c_ctx, ada_w, ada_b, norm_g, rk_mix, rk_w_in, rk_w0, rk_w1, rk_w2, rk_a0, rk_a1, rk_a2,
              rk_k_k, rk_k_a, rk_r_k, rk_ln_g, rk_ln_b, rk_w_out, rt_w_in, rt_decay_logit, rt_gn_g, rt_w_out, final_g):
    for i in range(DEPTH):
        last = i == DEPTH - 1
        shift, scale, gate = adaln(c, ada_w[i], ada_b[i])
        s_c, sc_c, g_c = adaln(c_ctx[None], ada_w[i], ada_b[i])
        h_lat = rmsnorm(x, norm_g[i]) * (1.0 + scale) + shift
        h_ctx = rmsnorm(ctx, norm_g[i]) * (1.0 + sc_c) + s_c
        j = i // N_MIXERS
        if i % N_MIXERS == 0:
            y_lat, y_ctx = rwkv7_mixer(h_lat, h_ctx, not last, rk_mix[j], rk_w_in[j], rk_w0[j], rk_w1[j], rk_w2[j],
                                       rk_a0[j], rk_a1[j], rk_a2[j], rk_k_k[j], rk_k_a[j], rk_r_k[j],
                                       rk_ln_g[j], rk_ln_b[j], rk_w_out[j])
        else:
            y_lat, y_ctx = retention_mixer(h_lat, h_ctx, not last, rt_w_in[j], rt_decay_logit[j], rt_gn_g[j], rt_w_out[j])
        x = x + gate * y_lat
        if not last:
            ctx = ctx + g_c * y_ctx
    return rmsnorm(x, final_g)
```

```python
import functools
import math

import jax
import jax.numpy as jnp
from jax import lax
from jax.experimental import pallas as pl
from jax.experimental.pallas import tpu as pltpu

F32 = jnp.float32
BF16 = jnp.bfloat16
HIGHEST = lax.Precision.HIGHEST

GRID_W = 64
EPS = 1e-6
RWKV_HEAD = 64
RWKV_GN_EPS = 64e-5
RET_HEADS = 8
ROPE_BASE = 10000.0

LANE = 128
MXU_DIM = 256
VMEM_LIMIT = 56 * 1024 * 1024

ROW_TILE = 256
OUT_TILE = 128
MM_TILE = 512
WKV_CHUNK = 64
WKV_LANES = MXU_DIM
RET_CHUNK = 256


def _cparams(sem, vmem=VMEM_LIMIT):
    return pltpu.CompilerParams(dimension_semantics=sem, vmem_limit_bytes=vmem)


def _mm(a, b):
    return jnp.dot(a.astype(BF16), b.astype(BF16), preferred_element_type=F32)


def _mm_nt(a, b):
    return lax.dot_general(a.astype(BF16), b.astype(BF16), (((1,), (1,)), ((), ())),
                           preferred_element_type=F32)


def _mm_tn(a, b):
    return lax.dot_general(a.astype(BF16), b.astype(BF16), (((0,), (0,)), ((), ())),
                           preferred_element_type=F32)


def _split2(x):
    hi = x.astype(BF16)
    lo = (x - hi.astype(F32)).astype(BF16)
    return hi, lo


def _split3(x):
    hi = x.astype(BF16)
    r = x - hi.astype(F32)
    mid = r.astype(BF16)
    lo = (r - mid.astype(F32)).astype(BF16)
    return hi, mid, lo


def _dot(a, b, dims):
    return lax.dot_general(a, b, dims, preferred_element_type=F32)


_NN = (((1,), (0,)), ((), ()))
_NT = (((1,), (1,)), ((), ()))
_TN = (((0,), (0,)), ((), ()))


def _mm3(a, b, dims=_NN):
    a1, a2 = _split2(a)
    b1, b2 = _split2(b)
    return _dot(a1, b1, dims) + (_dot(a1, b2, dims) + _dot(a2, b1, dims))


def _mm_exact_lhs(a_bf16, b):
    b1, b2, b3 = _split3(b)
    return _dot(a_bf16, b1, _NN) + (_dot(a_bf16, b2, _NN) + _dot(a_bf16, b3, _NN))


def _mm_exact_rhs(a, b_bf16):
    a1, a2, a3 = _split3(a)
    return _dot(a1, b_bf16, _NN) + (_dot(a2, b_bf16, _NN) + _dot(a3, b_bf16, _NN))


def _silu(x):
    return x * jax.nn.sigmoid(x)


def _norm_mod(xb, g, mod):
    y = xb * lax.rsqrt(jnp.mean(xb * xb, axis=-1, keepdims=True) + EPS) * g
    return y * (1.0 + mod[1:2]) + mod[0:1]


def _adaln_kernel(cv_ref, w_ref, b_ref, o_ref):
    s = _silu(cv_ref[...])
    o_ref[...] = jnp.dot(s, w_ref[...], precision=HIGHEST, preferred_element_type=F32) + b_ref[...]


def _adaln(cv, ada_w, ada_b):
    depth, d, d3 = ada_w.shape
    rows = cv.shape[0]
    tn = math.gcd(d3, 6 * LANE)
    return pl.pallas_call(
        _adaln_kernel,
        out_shape=jax.ShapeDtypeStruct((depth, rows, d3), F32),
        grid=(depth, d3 // tn),
        in_specs=[
            pl.BlockSpec((rows, d), lambda i, n: (0, 0)),
            pl.BlockSpec((None, d, tn), lambda i, n: (i, 0, n)),
            pl.BlockSpec((None, 1, tn), lambda i, n: (i, 0, n)),
        ],
        out_specs=pl.BlockSpec((None, rows, tn), lambda i, n: (i, 0, n)),
        compiler_params=_cparams(("parallel", "parallel")),
        name="adaln",
    )(cv, ada_w, ada_b.reshape(depth, 1, d3))


def _rk_pro_kernel(ctx_ref, x_ref, xp_ref, xn_ref, modc_ref, modl_ref, g_ref, mix_ref, o_ref,
                   *, n_lat_tiles):
    j = pl.program_id(1)
    g = g_ref[...]
    mix = mix_ref[...]
    t, d = x_ref.shape
    row = lax.broadcasted_iota(jnp.int32, (t, 1), 0)

    def emit(h, hs):
        delta = hs - h
        for p in range(mix.shape[0]):
            o_ref[p] = (h + delta * mix[p:p + 1]).astype(BF16)

    @pl.when(j == 0)
    def _():
        h = _norm_mod(ctx_ref[...], g, modc_ref[...])
        half = d // 2
        prev = jnp.where(row == 0, 0.0, pltpu.roll(h[:, :half], 1, 0))
        nxt = jnp.where(row == t - 1, 0.0, pltpu.roll(h[:, half:], t - 1, 0))
        emit(h, jnp.concatenate([prev, nxt], axis=-1))

    @pl.when(j > 0)
    def _():
        mod = modl_ref[...]
        h = _norm_mod(x_ref[...], g, mod)
        hp = _norm_mod(xp_ref[...], g, mod)
        hn = _norm_mod(xn_ref[...], g, mod)
        q = d // 4
        col = row % GRID_W
        left = jnp.where(col == 0, 0.0, pltpu.roll(h[:, :q], 1, 0))
        right = jnp.where(col == GRID_W - 1, 0.0, pltpu.roll(h[:, q:2 * q], t - 1, 0))
        up = jnp.concatenate([hp[:, 2 * q:3 * q], h[:t - GRID_W, 2 * q:3 * q]], axis=0)
        up = jnp.where((j == 1) & (row < GRID_W), 0.0, up)
        down = jnp.concatenate([h[GRID_W:, 3 * q:], hn[:, 3 * q:]], axis=0)
        down = jnp.where((j == n_lat_tiles) & (row >= t - GRID_W), 0.0, down)
        emit(h, jnp.concatenate([left, right, up, down], axis=-1))


def _rk_pro(x, ctx, mod0, norm_g0, mix):
    b, l, d = x.shape
    t = ROW_TILE
    n_lat = l // t
    sub = t // GRID_W
    n_rows = l // GRID_W
    xr = x.reshape(b, n_rows, GRID_W, d)
    nmix = mix.shape[0]
    lt = ctx.shape[1] + l
    return pl.pallas_call(
        functools.partial(_rk_pro_kernel, n_lat_tiles=n_lat),
        out_shape=jax.ShapeDtypeStruct((nmix, b, lt, d), BF16),
        grid=(b, n_lat + 1),
        in_specs=[
            pl.BlockSpec((None, t, d), lambda bi, j: (bi, 0, 0)),
            pl.BlockSpec((None, t, d), lambda bi, j: (bi, jnp.maximum(j - 1, 0), 0)),
            pl.BlockSpec((None, None, GRID_W, d),
                         lambda bi, j: (bi, jnp.maximum((j - 1) * sub - 1, 0), 0, 0)),
            pl.BlockSpec((None, None, GRID_W, d),
                         lambda bi, j: (bi, jnp.minimum(jnp.maximum(j, 1) * sub, n_rows - 1), 0, 0)),
            pl.BlockSpec((None, 3, d), lambda bi, j: (b, 0, 0)),
            pl.BlockSpec((None, 3, d), lambda bi, j: (bi, 0, 0)),
            pl.BlockSpec((1, d), lambda bi, j: (0, 0)),
            pl.BlockSpec((nmix, d), lambda bi, j: (0, 0)),
        ],
        out_specs=pl.BlockSpec((nmix, None, t, d), lambda bi, j: (0, bi, j, 0)),
        compiler_params=_cparams(("parallel", "parallel")),
        name="rk_pro",
    )(ctx, x, xr, xr, mod0, mod0, norm_g0.reshape(1, d), mix)


def _rk_proj_kernel(x_ref, w_ref, o_ref):
    o_ref[...] = jnp.dot(x_ref[...], w_ref[...], preferred_element_type=F32)


def _rk_proj(xm, w_in):
    nproj, d, _ = w_in.shape
    m = xm.shape[1]
    tm = MM_TILE
    return pl.pallas_call(
        _rk_proj_kernel,
        out_shape=jax.ShapeDtypeStruct((nproj, m, d), F32),
        grid=(nproj, m // tm),
        in_specs=[
            pl.BlockSpec((None, tm, d), lambda p, i: (p, i, 0)),
            pl.BlockSpec((None, d, d), lambda p, i: (p, 0, 0)),
        ],
        out_specs=pl.BlockSpec((None, tm, d), lambda p, i: (p, i, 0)),
        compiler_params=_cparams(("parallel", "parallel")),
        name="rk_proj",
    )(xm, w_in)


def _rk_lora_kernel(xw_ref, xa_ref, w1_ref, w2_ref, a1_ref, a2_ref, w0_ref, a0_ref, lw_ref, a_ref):
    xw = xw_ref[...]
    xa = xa_ref[...]
    for n in range(2):
        hw = jnp.tanh(jnp.dot(xw, w1_ref[n], preferred_element_type=F32))
        ww = w0_ref[n:n + 1, :] + _mm(hw, w2_ref[n])
        lw_ref[n] = -math.exp(-0.5) * jax.nn.sigmoid(ww)
        ha = jnp.dot(xa, a1_ref[n], preferred_element_type=F32)
        a_ref[n] = jax.nn.sigmoid(a0_ref[n:n + 1, :] + _mm(ha, a2_ref[n]))


def _rk_lora(xm, w1, w2, a1, a2, w0, a0):
    m, d = xm.shape[1:]
    r = w1.shape[-1]
    tm = MM_TILE
    full = lambda shape: pl.BlockSpec(shape, lambda i: (0,) * len(shape))
    out = jax.ShapeDtypeStruct((2, m, d), F32)
    return pl.pallas_call(
        _rk_lora_kernel,
        out_shape=(out, out),
        grid=(m // tm,),
        in_specs=[
            pl.BlockSpec((None, tm, d), lambda i: (4, i, 0)),
            pl.BlockSpec((None, tm, d), lambda i: (5, i, 0)),
            full((2, d, r)), full((2, r, d)), full((2, d, r)), full((2, r, d)),
            full((2, d)), full((2, d)),
        ],
        out_specs=(pl.BlockSpec((2, tm, d), lambda i: (0, i, 0)),
                   pl.BlockSpec((2, tm, d), lambda i: (0, i, 0))),
        compiler_params=_cparams(("parallel",)),
        name="rk_lora",
    )(xm, xm, w1, w2, a1, a2, w0, a0)


def _wkv_kernel(r_ref, k_ref, v_ref, lw_ref, a_ref, kk_ref, ka_ref, o_ref, t_ref,
                *, reverse, chunks_per_block):
    c = WKV_CHUNK
    hb = r_ref.shape[-1]
    nh = hb // RWKV_HEAD
    s = pl.program_id(2)

    @pl.when(s == 0)
    def _():
        t_ref[...] = jnp.zeros_like(t_ref)

    ri = lax.broadcasted_iota(jnp.int32, (c, c), 0)
    ci = lax.broadcasted_iota(jnp.int32, (c, c), 1)
    tri = ((ci >= ri) if reverse else (ci <= ri)).astype(BF16)
    rs = lax.broadcasted_iota(jnp.int32, (nh * c, nh * c), 0)
    cs = lax.broadcasted_iota(jnp.int32, (nh * c, nh * c), 1)
    same = (rs // c) == (cs // c)
    before = (cs % c > rs % c) if reverse else (cs % c < rs % c)
    strict = same & before
    incl = same & (before | (cs % c == rs % c))
    eye_s = (rs == cs).astype(F32)
    rl = lax.broadcasted_iota(jnp.int32, (hb, hb), 0)
    cl = lax.broadcasted_iota(jnp.int32, (hb, hb), 1)
    head_bd = (rl // RWKV_HEAD) == (cl // RWKV_HEAD)
    ones_bd = head_bd.astype(BF16)
    eye_l = rl == cl
    lane_head = lax.broadcasted_iota(jnp.int32, (1, hb), 1) // RWKV_HEAD
    kkp = kk_ref[...]
    kap = ka_ref[...]
    end_row = 0 if reverse else c - 1

    def stack(x):
        return jnp.concatenate([jnp.where(lane_head == h, x, 0.0) for h in range(nh)], axis=0)

    def unstack(x):
        acc = x[0:c]
        for h in range(1, nh):
            acc = acc + x[h * c:(h + 1) * c]
        return acc

    def chunk(i, carry):
        cidx = (chunks_per_block - 1 - i) if reverse else i
        rows = pl.ds(pl.multiple_of(cidx * c, c), c)
        r = r_ref[rows, :]
        k = k_ref[rows, :]
        v = v_ref[rows, :]
        lw = lw_ref[rows, :]
        a = a_ref[rows, :]

        b = _mm_exact_lhs(tri, lw)
        b_end = b[end_row:end_row + 1, :]
        e_pos = jnp.exp(b)
        e_neg = jnp.exp(-b)
        e_prev = jnp.exp(b - lw)
        e_end = jnp.exp(b_end - b)
        p_end = jnp.exp(b_end)

        kk = k * kkp
        ssq = _mm_exact_rhs(kk * kk, ones_bd)
        kappa = kk / jnp.maximum(jnp.sqrt(ssq), 1e-12)
        kd = k * (1.0 + (a - 1.0) * kap)
        beta = kappa * a

        kt_s = stack(kappa * e_prev)
        rt_s = stack(r * e_pos)
        bt_s = stack(beta * e_neg)
        kn_s = stack(kd * e_neg)
        v_s = stack(v)
        bbar = beta * e_end
        kbar = kd * e_end

        a_ab = jnp.where(strict, _mm3(kt_s, bt_s, _NT), 0.0)
        a_ak = jnp.where(strict, _mm_nt(kt_s, kn_s), 0.0)
        a_rb = jnp.where(incl, _mm_nt(rt_s, bt_s), 0.0)
        a_rk = jnp.where(incl, _mm_nt(rt_s, kn_s), 0.0)

        pw = -a_ab
        minv = eye_s + pw
        for _ in range(int(math.log2(c)) - 1):
            pw = _mm3(pw, pw)
            minv = minv + _mm3(minv, pw)

        av_s = _mm(a_ak, v_s)
        khat_s = _mm3(minv, kt_s)
        w_s = _mm(minv, av_s)
        rhat = unstack(rt_s - _mm(a_rb, khat_s))
        o_intra = unstack(_mm(a_rk, v_s) - _mm(a_rb, w_s))
        khat = unstack(khat_s)
        wmat = unstack(w_s)

        g = jnp.where(eye_l, p_end, 0.0) - jnp.where(head_bd, _mm3(bbar, khat, _TN), 0.0)
        hmat = jnp.where(head_bd, _mm_tn(kbar, v) - _mm_tn(bbar, wmat), 0.0)

        t0 = t_ref[...]
        o_ref[rows, :] = _mm(rhat, t0) + o_intra
        t_ref[...] = _mm3(g, t0) + hmat
        return carry

    lax.fori_loop(0, chunks_per_block, chunk, 0)


def _wkv(proj, lw, a, k_k, k_a, direction, n_ctx_blocks):
    _, b, lt, d = proj.shape
    rb = ROW_TILE
    hb = WKV_LANES
    nblk = lt // rb
    reverse = direction == 1

    if reverse:
        def blk(s):
            return jnp.where(s < n_ctx_blocks, n_ctx_blocks - 1 - s, nblk - 1 - (s - n_ctx_blocks))
    else:
        def blk(s):
            return s

    def tok(p):
        return pl.BlockSpec((None, None, rb, hb), lambda bi, hi, s: (p, bi, blk(s), hi))

    return pl.pallas_call(
        functools.partial(_wkv_kernel, reverse=reverse, chunks_per_block=rb // WKV_CHUNK),
        out_shape=jax.ShapeDtypeStruct((b, lt, d), F32),
        grid=(b, d // hb, nblk),
        in_specs=[tok(0), tok(1), tok(2), tok(direction), tok(direction),
                  pl.BlockSpec((1, hb), lambda bi, hi, s: (0, hi)),
                  pl.BlockSpec((1, hb), lambda bi, hi, s: (0, hi))],
        out_specs=pl.BlockSpec((None, rb, hb), lambda bi, hi, s: (bi, blk(s), hi)),
        scratch_shapes=[pltpu.VMEM((hb, hb), F32)],
        compiler_params=_cparams(("parallel", "parallel", "arbitrary")),
        name="wkv_bwd" if reverse else "wkv_fwd",
    )(proj, proj, proj, lw, a, k_k, k_a)


def _head_sums(x, ones_bd):
    w = ones_bd.shape[0]
    return jnp.concatenate(
        [_mm_exact_rhs(x[:, i:i + w], ones_bd) for i in range(0, x.shape[-1], w)], axis=-1)


def _rk_out_kernel(of_ref, ob_ref, r_ref, k_ref, v_ref, g_ref, a_ref, x_ref, ctx_ref,
                   modl_ref, modc_ref, modl1_ref, modc1_ref, ng1_ref,
                   ka_ref, rk_ref, lng_ref, lnb_ref, w_ref,
                   x1_ref, ctx1_ref, h1_ref, *, n_ctx_tiles):
    j = pl.program_id(1)
    hw = min(MXU_DIM, r_ref.shape[-1])
    rl = lax.broadcasted_iota(jnp.int32, (hw, hw), 0)
    cl = lax.broadcasted_iota(jnp.int32, (hw, hw), 1)
    ones_bd = ((rl // RWKV_HEAD) == (cl // RWKV_HEAD)).astype(BF16)
    inv_n = 1.0 / RWKV_HEAD

    o = of_ref[...] + ob_ref[...]
    mu = _head_sums(o, ones_bd) * inv_n
    oc = o - mu
    var = _head_sums(oc * oc, ones_bd) * inv_n
    on = oc * lax.rsqrt(var + RWKV_GN_EPS) * lng_ref[...] + lnb_ref[...]
    k = k_ref[...]
    ksum = k * (2.0 + (a_ref[0] + a_ref[1] - 2.0) * ka_ref[...])
    bonus = _head_sums(r_ref[...] * ksum * rk_ref[...], ones_bd) * v_ref[...]
    y = (on + bonus) * _silu(g_ref[...])
    out = _mm(y, w_ref[...])
    ng1 = ng1_ref[...]

    @pl.when(j < n_ctx_tiles)
    def _():
        rows = pl.ds(pl.multiple_of(j * OUT_TILE, OUT_TILE), OUT_TILE)
        new = ctx_ref[rows, :] + modc_ref[2:3, :] * out
        ctx1_ref[rows, :] = new
        h1_ref[...] = _norm_mod(new, ng1, modc1_ref[...]).astype(BF16)

    @pl.when(j >= n_ctx_tiles)
    def _():
        new = x_ref[...] + modl_ref[2:3, :] * out
        x1_ref[...] = new
        h1_ref[...] = _norm_mod(new, ng1, modl1_ref[...]).astype(BF16)


def _rk_out(o_f, o_b, proj, a, x, ctx, mod0, mod1, norm_g1, k_a, r_k, ln_g, ln_b, w_out):
    b, l, d = x.shape
    lc = ctx.shape[1]
    lt = lc + l
    t = OUT_TILE
    nct = lc // t
    tok = pl.BlockSpec((None, t, d), lambda bi, j: (bi, j, 0))
    ptok = lambda p: pl.BlockSpec((None, None, t, d), lambda bi, j: (p, bi, j, 0))
    vec = pl.BlockSpec((1, d), lambda bi, j: (0, 0))
    lat_blk = lambda bi, j: (bi, jnp.maximum(j - nct, 0), 0)
    mod_l = pl.BlockSpec((None, 3, d), lambda bi, j: (bi, 0, 0))
    mod_c = pl.BlockSpec((None, 3, d), lambda bi, j: (b, 0, 0))
    return pl.pallas_call(
        functools.partial(_rk_out_kernel, n_ctx_tiles=nct),
        out_shape=(jax.ShapeDtypeStruct((b, l, d), F32),
                   jax.ShapeDtypeStruct((b, lc, d), F32),
                   jax.ShapeDtypeStruct((b, lt, d), BF16)),
        grid=(b, lt // t),
        in_specs=[tok, tok, ptok(0), ptok(1), ptok(2), ptok(3),
                  pl.BlockSpec((2, None, t, d), lambda bi, j: (0, bi, j, 0)),
                  pl.BlockSpec((None, t, d), lat_blk),
                  pl.BlockSpec((None, lc, d), lambda bi, j: (bi, 0, 0)),
                  mod_l, mod_c, mod_l, mod_c, vec, vec, vec, vec, vec,
                  pl.BlockSpec((d, d), lambda bi, j: (0, 0))],
        out_specs=(pl.BlockSpec((None, t, d), lat_blk),
                   pl.BlockSpec((None, lc, d), lambda bi, j: (bi, 0, 0)),
                   pl.BlockSpec((None, t, d), lambda bi, j: (bi, j, 0))),
        compiler_params=_cparams(("parallel", "arbitrary")),
        name="rk_out",
    )(o_f, o_b, proj, proj, proj, proj, a, x, ctx, mod0, mod0, mod1, mod1,
      norm_g1.reshape(1, d), k_a, r_k, ln_g, ln_b, w_out)


def _rt_proj_kernel(h_ref, w_ref, cos_ref, sin_ref, o_ref, *, dk):
    n = pl.program_id(0)
    acc = jnp.dot(h_ref[...], w_ref[...], preferred_element_type=F32)
    tn = acc.shape[-1]
    half = dk // 2

    def rotate(scale):
        cos = cos_ref[...]
        sin = sin_ref[...]
        parts = []
        for h in range(tn // dk):
            x1 = acc[:, h * dk:h * dk + half]
            x2 = acc[:, h * dk + half:(h + 1) * dk]
            parts.append((x1 * cos - x2 * sin) * scale)
            parts.append((x1 * sin + x2 * cos) * scale)
        return jnp.concatenate(parts, axis=-1)

    @pl.when(n == 0)
    def _():
        o_ref[...] = rotate(1.0).astype(BF16)

    @pl.when(n == 1)
    def _():
        o_ref[...] = rotate(dk ** -0.5).astype(BF16)

    @pl.when(n > 1)
    def _():
        o_ref[...] = acc.astype(BF16)


def _rt_proj(h1, w_in, cos, sin, dk):
    b, lt, d = h1.shape
    n_out = w_in.shape[1]
    tn = d
    t = ROW_TILE
    half = dk // 2
    return pl.pallas_call(
        functools.partial(_rt_proj_kernel, dk=dk),
        out_shape=jax.ShapeDtypeStruct((b, lt, n_out), BF16),
        grid=(n_out // tn, b, lt // t),
        in_specs=[
            pl.BlockSpec((None, t, d), lambda n, bi, j: (bi, j, 0)),
            pl.BlockSpec((d, tn), lambda n, bi, j: (0, n)),
            pl.BlockSpec((t, half), lambda n, bi, j: (j, 0)),
            pl.BlockSpec((t, half), lambda n, bi, j: (j, 0)),
        ],
        out_specs=pl.BlockSpec((None, t, tn), lambda n, bi, j: (bi, j, n)),
        compiler_params=_cparams(("parallel", "parallel", "parallel")),
        name="rt_proj",
    )(h1, w_in, cos, sin)


def _ret_kernel(lg_ref, q_ref, k_ref, v_ref, o_ref, r_ref, *, reverse):
    c = q_ref.shape[0]
    s = pl.program_id(2)

    @pl.when(s == 0)
    def _():
        r_ref[...] = jnp.zeros_like(r_ref)

    x = lg_ref[0]
    lg = jnp.minimum(x, 0.0) - jnp.log1p(jnp.exp(-jnp.abs(x)))
    lg_s = lg[:, 0:1]
    ri = lax.broadcasted_iota(jnp.int32, (c, c), 0)
    ci = lax.broadcasted_iota(jnp.int32, (c, c), 1)
    diff = ((ci - ri) if reverse else (ri - ci)).astype(F32)
    mask = (diff > 0) if reverse else (diff >= 0)
    dm = jnp.where(mask, jnp.exp(jnp.where(mask, diff, 0.0) * lg_s), 0.0)
    pos = lax.broadcasted_iota(jnp.int32, (c, 1), 0).astype(F32)
    p = (c - 1 - pos) if reverse else pos
    dq = jnp.exp((p + 1.0) * lg_s)
    dkey = jnp.exp((c - 1.0 - p) * lg_s)
    dchunk = jnp.exp(float(c) * lg_s)

    q = q_ref[...]
    k = k_ref[...]
    v = v_ref[...]
    r0 = r_ref[...]
    sc = lax.dot_general(q, k, _NT, preferred_element_type=F32) * dm
    o_ref[...] = _mm(sc, v) + _mm(q, r0) * dq
    kd = (k.astype(F32) * dkey).astype(BF16)
    r_ref[...] = r0 * dchunk + lax.dot_general(kd, v, _TN, preferred_element_type=F32)


def _ret(qkvg, decay_logit_dir, direction, n_ctx_blocks, dk, dv):
    b, lt, _ = qkvg.shape
    c = RET_CHUNK
    nblk = lt // c
    nh = RET_HEADS
    reverse = direction == 1
    if reverse:
        def blk(s):
            return jnp.where(s < n_ctx_blocks, n_ctx_blocks - 1 - s, nblk - 1 - (s - n_ctx_blocks))
    else:
        def blk(s):
            return s
    koff = (nh * dk) // dk
    voff = (2 * nh * dk) // dv
    lg = jnp.broadcast_to(decay_logit_dir.astype(F32)[:, None, None], (nh, 1, LANE))
    return pl.pallas_call(
        functools.partial(_ret_kernel, reverse=reverse),
        out_shape=jax.ShapeDtypeStruct((b, lt, nh * dv), F32),
        grid=(b, nh, nblk),
        in_specs=[
            pl.BlockSpec((1, 1, LANE), lambda bi, h, s: (h, 0, 0)),
            pl.BlockSpec((None, c, dk), lambda bi, h, s: (bi, blk(s), h)),
            pl.BlockSpec((None, c, dk), lambda bi, h, s: (bi, blk(s), koff + h)),
            pl.BlockSpec((None, c, dv), lambda bi, h, s: (bi, blk(s), voff + h)),
        ],
        out_specs=pl.BlockSpec((None, c, dv), lambda bi, h, s: (bi, blk(s), h)),
        scratch_shapes=[pltpu.VMEM((dk, dv), F32)],
        compiler_params=_cparams(("parallel", "parallel", "arbitrary")),
        name="ret_bwd" if reverse else "ret_fwd",
    )(lg, qkvg, qkvg, qkvg)


def _rt_out_kernel(of_ref, ob_ref, g_ref, x_ref, mod_ref, gng_ref, fg_ref, w_ref, o_ref, *, dv):
    o = of_ref[...] + ob_ref[...]
    parts = []
    for h in range(o.shape[-1] // dv):
        oh = o[:, h * dv:(h + 1) * dv]
        parts.append(oh * lax.rsqrt(jnp.mean(oh * oh, axis=-1, keepdims=True) + EPS))
    on = jnp.concatenate(parts, axis=-1)
    y = on * gng_ref[...] * _silu(g_ref[...].astype(F32))
    new = x_ref[...] + mod_ref[2:3, :] * _mm(y, w_ref[...])
    o_ref[...] = new * lax.rsqrt(jnp.mean(new * new, axis=-1, keepdims=True) + EPS) * fg_ref[...]


def _rt_out(o_f, o_b, qkvg, x1, mod1, gn_g, final_g, w_out, n_ctx_tiles, dv):
    b, l, d = x1.shape
    vd = w_out.shape[0]
    t = OUT_TILE
    goff = (qkvg.shape[-1] - vd) // vd
    return pl.pallas_call(
        functools.partial(_rt_out_kernel, dv=dv),
        out_shape=jax.ShapeDtypeStruct((b, l, d), F32),
        grid=(b, l // t),
        in_specs=[
            pl.BlockSpec((None, t, vd), lambda bi, j: (bi, j + n_ctx_tiles, 0)),
            pl.BlockSpec((None, t, vd), lambda bi, j: (bi, j + n_ctx_tiles, 0)),
            pl.BlockSpec((None, t, vd), lambda bi, j: (bi, j + n_ctx_tiles, goff)),
            pl.BlockSpec((None, t, d), lambda bi, j: (bi, j, 0)),
            pl.BlockSpec((None, 3, d), lambda bi, j: (bi, 0, 0)),
            pl.BlockSpec((1, vd), lambda bi, j: (0, 0)),
            pl.BlockSpec((1, d), lambda bi, j: (0, 0)),
            pl.BlockSpec((vd, d), lambda bi, j: (0, 0)),
        ],
        out_specs=pl.BlockSpec((None, t, d), lambda bi, j: (bi, j, 0)),
        compiler_params=_cparams(("parallel", "parallel")),
        name="rt_out",
    )(o_f, o_b, qkvg, x1, mod1, gn_g.reshape(1, vd), final_g.reshape(1, d), w_out)


def _rope_tables(l, lc, dk):
    t = jnp.arange(l)
    row = (t // GRID_W).astype(F32)
    col = (t % GRID_W).astype(F32)
    nf = dk // 4
    inv = ROPE_BASE ** (-jnp.arange(nf, dtype=F32) / nf)
    ang = jnp.concatenate([row[:, None] * inv, col[:, None] * inv], axis=-1)
    cos = jnp.concatenate([jnp.ones((lc, dk // 2), F32), jnp.cos(ang)], axis=0)
    sin = jnp.concatenate([jnp.zeros((lc, dk // 2), F32), jnp.sin(ang)], axis=0)
    return cos, sin


def kernel(x, c, ctx, c_ctx, ada_w, ada_b, norm_g, rk_mix, rk_w_in, rk_w0, rk_w1, rk_w2, rk_a0, rk_a1,
           rk_a2, rk_k_k, rk_k_a, rk_r_k, rk_ln_g, rk_ln_b, rk_w_out, rt_w_in, rt_decay_logit, rt_gn_g,
           rt_w_out, final_g):
    b, l, d = x.shape
    lc = ctx.shape[1]
    lt = lc + l
    assert lc == ROW_TILE and l % ROW_TILE == 0 and ROW_TILE % GRID_W == 0
    assert d % WKV_LANES == 0 and (b * lt) % MM_TILE == 0
    assert ada_w.shape[0] == 2 and rk_w_in.shape[0] == 1 and rt_w_in.shape[0] == 1
    dk = d // RET_HEADS
    dv = rt_w_out.shape[1] // RET_HEADS

    n_mod = -(-(b + 1) // 8) * 8
    cv = jnp.concatenate([c, c_ctx[None], jnp.zeros((n_mod - b - 1, d), F32)], axis=0)
    mod = _adaln(cv, ada_w, ada_b).reshape(2, n_mod, 3, d)

    xm = _rk_pro(x, ctx, mod[0], norm_g[0], rk_mix[0])
    xm2 = xm.reshape(6, b * lt, d)
    proj = _rk_proj(xm2, rk_w_in[0].astype(BF16)).reshape(4, b, lt, d)
    lw, a = _rk_lora(xm2, rk_w1[0].astype(BF16), rk_w2[0].astype(BF16), rk_a1[0].astype(BF16),
                     rk_a2[0].astype(BF16), rk_w0[0], rk_a0[0])
    lw = lw.reshape(2, b, lt, d)
    a = a.reshape(2, b, lt, d)
    k_k = rk_k_k[0].reshape(1, d)
    k_a = rk_k_a[0].reshape(1, d)
    n_ctx_blocks = lc // ROW_TILE
    o_f = _wkv(proj, lw, a, k_k, k_a, 0, n_ctx_blocks)
    o_b = _wkv(proj, lw, a, k_k, k_a, 1, n_ctx_blocks)
    x1, ctx1, h1 = _rk_out(o_f, o_b, proj, a, x, ctx, mod[0], mod[1], norm_g[1], k_a,
                           rk_r_k[0].reshape(1, d), rk_ln_g[0].reshape(1, d), rk_ln_b[0].reshape(1, d),
                           rk_w_out[0].astype(BF16))
    del ctx1

    cos, sin = _rope_tables(l, lc, dk)
    qkvg = _rt_proj(h1, rt_w_in[0].astype(BF16), cos, sin, dk)
    r_f = _ret(qkvg, rt_decay_logit[0, 0], 0, lc // RET_CHUNK, dk, dv)
    r_b = _ret(qkvg, rt_decay_logit[0, 1], 1, lc // RET_CHUNK, dk, dv)
    return _rt_out(r_f, r_b, qkvg, x1, mod[1], rt_gn_g[0], final_g, rt_w_out[0].astype(BF16),
                   lc // OUT_TILE, dv)
```

```python
import functools
import math

import jax
import jax.numpy as jnp
from jax import lax
from jax.experimental import pallas as pl
from jax.experimental.pallas import tpu as pltpu

F32 = jnp.float32
BF16 = jnp.bfloat16
HIGHEST = lax.Precision.HIGHEST

GRID_W = 64
EPS = 1e-6
RWKV_HEAD = 64
RWKV_GN_EPS = 64e-5
RET_HEADS = 8
ROPE_BASE = 10000.0

LANE = 128
MXU_DIM = 256
VMEM_LIMIT = 56 * 1024 * 1024

ROW_TILE = 256
OUT_TILE = 128
MM_TILE = 512
WKV_CHUNK = 64
WKV_GROUP = MXU_DIM
WKV_LANES = 1024
RET_CHUNK = 256


def _cparams(sem, vmem=VMEM_LIMIT):
    return pltpu.CompilerParams(dimension_semantics=sem, vmem_limit_bytes=vmem)


def _mm(a, b):
    return jnp.dot(a.astype(BF16), b.astype(BF16), preferred_element_type=F32)


def _mm_nt(a, b):
    return lax.dot_general(a.astype(BF16), b.astype(BF16), (((1,), (1,)), ((), ())),
                           preferred_element_type=F32)


def _mm_tn(a, b):
    return lax.dot_general(a.astype(BF16), b.astype(BF16), (((0,), (0,)), ((), ())),
                           preferred_element_type=F32)


def _split2(x):
    hi = x.astype(BF16)
    lo = (x - hi.astype(F32)).astype(BF16)
    return hi, lo


def _split3(x):
    hi = x.astype(BF16)
    r = x - hi.astype(F32)
    mid = r.astype(BF16)
    lo = (r - mid.astype(F32)).astype(BF16)
    return hi, mid, lo


def _dot(a, b, dims):
    return lax.dot_general(a, b, dims, preferred_element_type=F32)


_NN = (((1,), (0,)), ((), ()))
_NT = (((1,), (1,)), ((), ()))
_TN = (((0,), (0,)), ((), ()))


def _mm3(a, b, dims=_NN):
    a1, a2 = _split2(a)
    b1, b2 = _split2(b)
    return _dot(a1, b1, dims) + (_dot(a1, b2, dims) + _dot(a2, b1, dims))


def _mm_exact_lhs(a_bf16, b):
    b1, b2, b3 = _split3(b)
    return _dot(a_bf16, b1, _NN) + (_dot(a_bf16, b2, _NN) + _dot(a_bf16, b3, _NN))


def _mm_exact_rhs(a, b_bf16):
    a1, a2, a3 = _split3(a)
    return _dot(a1, b_bf16, _NN) + (_dot(a2, b_bf16, _NN) + _dot(a3, b_bf16, _NN))


def _silu(x):
    return x * jax.nn.sigmoid(x)


def _norm_mod(xb, g, mod):
    y = xb * lax.rsqrt(jnp.mean(xb * xb, axis=-1, keepdims=True) + EPS) * g
    return y * (1.0 + mod[1:2]) + mod[0:1]


def _adaln_kernel(cv_ref, w_ref, b_ref, o_ref):
    s = _silu(cv_ref[...])
    o_ref[...] = jnp.dot(s, w_ref[...], precision=HIGHEST, preferred_element_type=F32) + b_ref[...]


def _adaln(cv, ada_w, ada_b):
    depth, d, d3 = ada_w.shape
    rows = cv.shape[0]
    tn = math.gcd(d3, 6 * LANE)
    return pl.pallas_call(
        _adaln_kernel,
        out_shape=jax.ShapeDtypeStruct((depth, rows, d3), F32),
        grid=(depth, d3 // tn),
        in_specs=[
            pl.BlockSpec((rows, d), lambda i, n: (0, 0)),
            pl.BlockSpec((None, d, tn), lambda i, n: (i, 0, n)),
            pl.BlockSpec((None, 1, tn), lambda i, n: (i, 0, n)),
        ],
        out_specs=pl.BlockSpec((None, rows, tn), lambda i, n: (i, 0, n)),
        compiler_params=_cparams(("parallel", "parallel")),
        name="adaln",
    )(cv, ada_w, ada_b.reshape(depth, 1, d3))


def _rk_pro_kernel(ctx_ref, x_ref, xp_ref, xn_ref, modc_ref, modl_ref, g_ref, mix_ref, o_ref,
                   *, n_lat_tiles):
    j = pl.program_id(1)
    g = g_ref[...]
    mix = mix_ref[...]
    t, d = x_ref.shape
    row = lax.broadcasted_iota(jnp.int32, (t, 1), 0)

    def emit(h, hs):
        delta = hs - h
        for p in range(mix.shape[0]):
            o_ref[p] = (h + delta * mix[p:p + 1]).astype(BF16)

    @pl.when(j == 0)
    def _():
        h = _norm_mod(ctx_ref[...], g, modc_ref[...])
        half = d // 2
        prev = jnp.where(row == 0, 0.0, pltpu.roll(h[:, :half], 1, 0))
        nxt = jnp.where(row == t - 1, 0.0, pltpu.roll(h[:, half:], t - 1, 0))
        emit(h, jnp.concatenate([prev, nxt], axis=-1))

    @pl.when(j > 0)
    def _():
        mod = modl_ref[...]
        h = _norm_mod(x_ref[...], g, mod)
        hp = _norm_mod(xp_ref[...], g, mod)
        hn = _norm_mod(xn_ref[...], g, mod)
        q = d // 4
        col = row % GRID_W
        left = jnp.where(col == 0, 0.0, pltpu.roll(h[:, :q], 1, 0))
        right = jnp.where(col == GRID_W - 1, 0.0, pltpu.roll(h[:, q:2 * q], t - 1, 0))
        up = jnp.concatenate([hp[:, 2 * q:3 * q], h[:t - GRID_W, 2 * q:3 * q]], axis=0)
        up = jnp.where((j == 1) & (row < GRID_W), 0.0, up)
        down = jnp.concatenate([h[GRID_W:, 3 * q:], hn[:, 3 * q:]], axis=0)
        down = jnp.where((j == n_lat_tiles) & (row >= t - GRID_W), 0.0, down)
        emit(h, jnp.concatenate([left, right, up, down], axis=-1))


def _rk_pro(x, ctx, mod0, norm_g0, mix):
    b, l, d = x.shape
    t = ROW_TILE
    n_lat = l // t
    sub = t // GRID_W
    n_rows = l // GRID_W
    xr = x.reshape(b, n_rows, GRID_W, d)
    nmix = mix.shape[0]
    lt = ctx.shape[1] + l
    return pl.pallas_call(
        functools.partial(_rk_pro_kernel, n_lat_tiles=n_lat),
        out_shape=jax.ShapeDtypeStruct((nmix, b, lt, d), BF16),
        grid=(b, n_lat + 1),
        in_specs=[
            pl.BlockSpec((None, t, d), lambda bi, j: (bi, 0, 0)),
            pl.BlockSpec((None, t, d), lambda bi, j: (bi, jnp.maximum(j - 1, 0), 0)),
            pl.BlockSpec((None, None, GRID_W, d),
                         lambda bi, j: (bi, jnp.maximum((j - 1) * sub - 1, 0), 0, 0)),
            pl.BlockSpec((None, None, GRID_W, d),
                         lambda bi, j: (bi, jnp.minimum(jnp.maximum(j, 1) * sub, n_rows - 1), 0, 0)),
            pl.BlockSpec((None, 3, d), lambda bi, j: (b, 0, 0)),
            pl.BlockSpec((None, 3, d), lambda bi, j: (bi, 0, 0)),
            pl.BlockSpec((1, d), lambda bi, j: (0, 0)),
            pl.BlockSpec((nmix, d), lambda bi, j: (0, 0)),
        ],
        out_specs=pl.BlockSpec((nmix, None, t, d), lambda bi, j: (0, bi, j, 0)),
        compiler_params=_cparams(("parallel", "parallel")),
        name="rk_pro",
    )(ctx, x, xr, xr, mod0, mod0, norm_g0.reshape(1, d), mix)


def _rk_proj_kernel(x_ref, w_ref, o_ref):
    o_ref[...] = jnp.dot(x_ref[...], w_ref[...], preferred_element_type=F32)


def _rk_proj(xm, w_in):
    nproj, d, _ = w_in.shape
    m = xm.shape[1]
    tm = MM_TILE
    return pl.pallas_call(
        _rk_proj_kernel,
        out_shape=jax.ShapeDtypeStruct((nproj, m, d), F32),
        grid=(nproj, m // tm),
        in_specs=[
            pl.BlockSpec((None, tm, d), lambda p, i: (p, i, 0)),
            pl.BlockSpec((None, d, d), lambda p, i: (p, 0, 0)),
        ],
        out_specs=pl.BlockSpec((None, tm, d), lambda p, i: (p, i, 0)),
        compiler_params=_cparams(("parallel", "parallel")),
        name="rk_proj",
    )(xm, w_in)


def _rk_lora_kernel(xw_ref, xa_ref, w1_ref, w2_ref, a1_ref, a2_ref, w0_ref, a0_ref, lw_ref, a_ref):
    xw = xw_ref[...]
    xa = xa_ref[...]
    for n in range(2):
        hw = jnp.tanh(jnp.dot(xw, w1_ref[n], preferred_element_type=F32))
        ww = w0_ref[n:n + 1, :] + _mm(hw, w2_ref[n])
        lw_ref[n] = -math.exp(-0.5) * jax.nn.sigmoid(ww)
        ha = jnp.dot(xa, a1_ref[n], preferred_element_type=F32)
        a_ref[n] = jax.nn.sigmoid(a0_ref[n:n + 1, :] + _mm(ha, a2_ref[n]))


def _rk_lora(xm, w1, w2, a1, a2, w0, a0):
    m, d = xm.shape[1:]
    r = w1.shape[-1]
    tm = MM_TILE
    full = lambda shape: pl.BlockSpec(shape, lambda i: (0,) * len(shape))
    out = jax.ShapeDtypeStruct((2, m, d), F32)
    return pl.pallas_call(
        _rk_lora_kernel,
        out_shape=(out, out),
        grid=(m // tm,),
        in_specs=[
            pl.BlockSpec((None, tm, d), lambda i: (4, i, 0)),
            pl.BlockSpec((None, tm, d), lambda i: (5, i, 0)),
            full((2, d, r)), full((2, r, d)), full((2, d, r)), full((2, r, d)),
            full((2, d)), full((2, d)),
        ],
        out_specs=(pl.BlockSpec((2, tm, d), lambda i: (0, i, 0)),
                   pl.BlockSpec((2, tm, d), lambda i: (0, i, 0))),
        compiler_params=_cparams(("parallel",)),
        name="rk_lora",
    )(xm, xm, w1, w2, a1, a2, w0, a0)


def _wkv_kernel(r_ref, k_ref, v_ref, lw_ref, a_ref, kk_ref, ka_ref, o_ref, t_ref,
                *, reverse, chunks_per_block):
    c = WKV_CHUNK
    gw = WKV_GROUP
    nq = r_ref.shape[-1] // gw
    nh = gw // RWKV_HEAD
    hb = gw
    s = pl.program_id(2)

    @pl.when(s == 0)
    def _():
        t_ref[...] = jnp.zeros_like(t_ref)

    ri = lax.broadcasted_iota(jnp.int32, (c, c), 0)
    ci = lax.broadcasted_iota(jnp.int32, (c, c), 1)
    tri = ((ci >= ri) if reverse else (ci <= ri)).astype(BF16)
    rw = lax.broadcasted_iota(jnp.int32, (c, nh * c), 0)
    cw = lax.broadcasted_iota(jnp.int32, (c, nh * c), 1) % c
    strict_w = (cw > rw) if reverse else (cw < rw)
    incl_w = strict_w | (cw == rw)
    eye_w = (cw == rw).astype(F32)
    rs = lax.broadcasted_iota(jnp.int32, (nh * c, nh * c), 0)
    cs = lax.broadcasted_iota(jnp.int32, (nh * c, nh * c), 1)
    bd_mask = ((rs // c) == (cs // c)).astype(BF16)
    rl = lax.broadcasted_iota(jnp.int32, (hb, hb), 0)
    cl = lax.broadcasted_iota(jnp.int32, (hb, hb), 1)
    head_bd = (rl // RWKV_HEAD) == (cl // RWKV_HEAD)
    ones_bd = head_bd.astype(BF16)
    lane_head = lax.broadcasted_iota(jnp.int32, (1, hb), 1) // RWKV_HEAD
    lane_masks = [(lane_head == h).astype(BF16) for h in range(nh)]
    end_row = 0 if reverse else c - 1

    def stack(x):
        xb = x.astype(BF16)
        return jnp.concatenate([xb * m for m in lane_masks], axis=0)

    def block_diag(xw):
        return jnp.concatenate([xw.astype(BF16)] * nh, axis=0) * bd_mask

    def prepass(cidx, q, out):
        rows = pl.ds(cidx * c, c)
        lanes = pl.ds(q * gw, gw)
        r = r_ref[rows, lanes]
        k = k_ref[rows, lanes]
        v = v_ref[rows, lanes]
        lw = lw_ref[rows, lanes]
        a = a_ref[rows, lanes]
        kap = ka_ref[:, lanes]

        lw_hi, lw_lo = _split2(lw)
        b = _dot(tri, lw_hi, _NN) + _dot(tri, lw_lo, _NN)
        kk = k * kk_ref[:, lanes]
        ssq = ssq_blocks[q][cidx * c:(cidx + 1) * c]
        yield
        b_end = b[end_row:end_row + 1, :]
        e_pos = jnp.exp(b)
        e_neg = jnp.exp(-b)
        e_prev = jnp.exp(b - lw)
        e_end = jnp.exp(b_end - b)
        p_end = jnp.exp(b_end)

        kappa = kk / jnp.maximum(jnp.sqrt(ssq), 1e-12)
        kd = k * (1.0 + (a - 1.0) * kap)
        beta = kappa * a

        kt = kappa * e_prev
        rt = r * e_pos
        kt_s = stack(kt)
        bt_s = stack(beta * e_neg)
        kn_s = stack(kd * e_neg)
        v_s = stack(v)
        bbar = beta * e_end
        kbar = kd * e_end

        lhs = jnp.concatenate([kt, rt], axis=0)
        ab = _mm_nt(lhs, bt_s)
        ak = _mm_nt(lhs, kn_s)
        yield
        a_ab = jnp.where(strict_w, ab[:c], 0.0)
        a_rb = jnp.where(incl_w, ab[c:], 0.0)
        a_ak = jnp.where(strict_w, ak[:c], 0.0)
        a_rk = jnp.where(incl_w, ak[c:], 0.0)

        pw = -a_ab
        minv = eye_w + pw
        pw = _mm(pw, block_diag(pw))
        intra_v = _mm(jnp.concatenate([a_ak, a_rk], axis=0), v_s)
        av = intra_v[:c]
        ark_v = intra_v[c:]
        yield
        n_sq = int(math.log2(c)) - 1
        for i in range(n_sq):
            both = _mm(jnp.concatenate([minv, pw], axis=0) if i < n_sq - 1 else minv, block_diag(pw))
            minv = minv + both[:c]
            pw = both[c:]
            yield

        khat = _mm(minv, kt_s)
        wmat = _mm(minv, stack(av))
        yield
        rhat = rt - _mm(a_rb, stack(khat))
        o_intra = ark_v - _mm(a_rb, stack(wmat))
        readout = jnp.concatenate([rhat, khat], axis=0).astype(BF16)
        keys = jnp.concatenate([kbar, bbar], axis=0).astype(BF16)
        out[(cidx, q)] = (readout, o_intra, wmat, v.astype(BF16), keys, p_end)

    order = list(range(chunks_per_block))
    if reverse:
        order.reverse()
    factors = {}
    ssq_blocks = []
    for q in range(nq):
        kk_blk = k_ref[:, pl.ds(q * gw, gw)] * kk_ref[:, pl.ds(q * gw, gw)]
        ssq_blocks.append(_mm(kk_blk * kk_blk, ones_bd))
    pending = [prepass(cidx, q, factors) for cidx in order for q in range(nq)]
    while pending:
        pending = [gen for gen in pending if next(gen, "done") != "done"]
    for cidx in order:
        rows = pl.ds(cidx * c, c)
        reads = []
        for q in range(nq):
            readout = factors[(cidx, q)][0]
            s_hi, s_lo = _split2(t_ref[q])
            reads.append(_dot(readout, s_hi, _NT) + _dot(readout, s_lo, _NT))
        for q in range(nq):
            _, o_intra, wmat, vb, keys, p_end = factors[(cidx, q)]
            o_ref[rows, pl.ds(q * gw, gw)] = reads[q][:c] + o_intra
            u = -(reads[q][c:] + wmat)
            vu = jnp.concatenate([vb, u.astype(BF16)], axis=0)
            t_ref[q] = t_ref[q] * p_end + jnp.where(head_bd, _dot(vu, keys, _TN), 0.0)


def _wkv(proj, lw, a, k_k, k_a, direction, n_ctx_blocks):
    _, b, lt, d = proj.shape
    rb = ROW_TILE
    hb = min(WKV_LANES, d)
    nblk = lt // rb
    reverse = direction == 1

    if reverse:
        def blk(s):
            return jnp.where(s < n_ctx_blocks, n_ctx_blocks - 1 - s, nblk - 1 - (s - n_ctx_blocks))
    else:
        def blk(s):
            return s

    def tok(p):
        return pl.BlockSpec((None, None, rb, hb), lambda bi, hi, s: (p, bi, blk(s), hi))

    return pl.pallas_call(
        functools.partial(_wkv_kernel, reverse=reverse, chunks_per_block=rb // WKV_CHUNK),
        out_shape=jax.ShapeDtypeStruct((b, lt, d), F32),
        grid=(b, d // hb, nblk),
        in_specs=[tok(0), tok(1), tok(2), tok(direction), tok(direction),
                  pl.BlockSpec((1, hb), lambda bi, hi, s: (0, hi)),
                  pl.BlockSpec((1, hb), lambda bi, hi, s: (0, hi))],
        out_specs=pl.BlockSpec((None, rb, hb), lambda bi, hi, s: (bi, blk(s), hi)),
        scratch_shapes=[pltpu.VMEM((hb // WKV_GROUP, WKV_GROUP, WKV_GROUP), F32)],
        compiler_params=_cparams(("parallel", "parallel", "arbitrary")),
        name="wkv_bwd" if reverse else "wkv_fwd",
    )(proj, proj, proj, lw, a, k_k, k_a)


def _head_sums(x, ones_bd):
    w = ones_bd.shape[0]
    return jnp.concatenate(
        [_mm_exact_rhs(x[:, i:i + w], ones_bd) for i in range(0, x.shape[-1], w)], axis=-1)


def _rk_out_kernel(of_ref, ob_ref, r_ref, k_ref, v_ref, g_ref, a_ref, x_ref, ctx_ref,
                   modl_ref, modc_ref, modl1_ref, modc1_ref, ng1_ref,
                   ka_ref, rk_ref, lng_ref, lnb_ref, w_ref,
                   x1_ref, ctx1_ref, h1_ref, *, n_ctx_tiles):
    j = pl.program_id(1)
    hw = min(MXU_DIM, r_ref.shape[-1])
    rl = lax.broadcasted_iota(jnp.int32, (hw, hw), 0)
    cl = lax.broadcasted_iota(jnp.int32, (hw, hw), 1)
    ones_bd = ((rl // RWKV_HEAD) == (cl // RWKV_HEAD)).astype(BF16)
    inv_n = 1.0 / RWKV_HEAD

    o = of_ref[...] + ob_ref[...]
    mu = _head_sums(o, ones_bd) * inv_n
    oc = o - mu
    var = _head_sums(oc * oc, ones_bd) * inv_n
    on = oc * lax.rsqrt(var + RWKV_GN_EPS) * lng_ref[...] + lnb_ref[...]
    k = k_ref[...]
    ksum = k * (2.0 + (a_ref[0] + a_ref[1] - 2.0) * ka_ref[...])
    bonus = _head_sums(r_ref[...] * ksum * rk_ref[...], ones_bd) * v_ref[...]
    y = (on + bonus) * _silu(g_ref[...])
    out = _mm(y, w_ref[...])
    ng1 = ng1_ref[...]

    @pl.when(j < n_ctx_tiles)
    def _():
        rows = pl.ds(pl.multiple_of(j * OUT_TILE, OUT_TILE), OUT_TILE)
        new = ctx_ref[rows, :] + modc_ref[2:3, :] * out
        ctx1_ref[rows, :] = new
        h1_ref[...] = _norm_mod(new, ng1, modc1_ref[...]).astype(BF16)

    @pl.when(j >= n_ctx_tiles)
    def _():
        new = x_ref[...] + modl_ref[2:3, :] * out
        x1_ref[...] = new
        h1_ref[...] = _norm_mod(new, ng1, modl1_ref[...]).astype(BF16)


def _rk_out(o_f, o_b, proj, a, x, ctx, mod0, mod1, norm_g1, k_a, r_k, ln_g, ln_b, w_out):
    b, l, d = x.shape
    lc = ctx.shape[1]
    lt = lc + l
    t = OUT_TILE
    nct = lc // t
    tok = pl.BlockSpec((None, t, d), lambda bi, j: (bi, j, 0))
    ptok = lambda p: pl.BlockSpec((None, None, t, d), lambda bi, j: (p, bi, j, 0))
    vec = pl.BlockSpec((1, d), lambda bi, j: (0, 0))
    lat_blk = lambda bi, j: (bi, jnp.maximum(j - nct, 0), 0)
    mod_l = pl.BlockSpec((None, 3, d), lambda bi, j: (bi, 0, 0))
    mod_c = pl.BlockSpec((None, 3, d), lambda bi, j: (b, 0, 0))
    return pl.pallas_call(
        functools.partial(_rk_out_kernel, n_ctx_tiles=nct),
        out_shape=(jax.ShapeDtypeStruct((b, l, d), F32),
                   jax.ShapeDtypeStruct((b, lc, d), F32),
                   jax.ShapeDtypeStruct((b, lt, d), BF16)),
        grid=(b, lt // t),
        in_specs=[tok, tok, ptok(0), ptok(1), ptok(2), ptok(3),
                  pl.BlockSpec((2, None, t, d), lambda bi, j: (0, bi, j, 0)),
                  pl.BlockSpec((None, t, d), lat_blk),
                  pl.BlockSpec((None, lc, d), lambda bi, j: (bi, 0, 0)),
                  mod_l, mod_c, mod_l, mod_c, vec, vec, vec, vec, vec,
                  pl.BlockSpec((d, d), lambda bi, j: (0, 0))],
        out_specs=(pl.BlockSpec((None, t, d), lat_blk),
                   pl.BlockSpec((None, lc, d), lambda bi, j: (bi, 0, 0)),
                   pl.BlockSpec((None, t, d), lambda bi, j: (bi, j, 0))),
        compiler_params=_cparams(("parallel", "arbitrary")),
        name="rk_out",
    )(o_f, o_b, proj, proj, proj, proj, a, x, ctx, mod0, mod0, mod1, mod1,
      norm_g1.reshape(1, d), k_a, r_k, ln_g, ln_b, w_out)


def _rt_proj_kernel(h_ref, w_ref, cos_ref, sin_ref, o_ref, *, dk):
    n = pl.program_id(0)
    acc = jnp.dot(h_ref[...], w_ref[...], preferred_element_type=F32)
    tn = acc.shape[-1]
    half = dk // 2

    def rotate(scale):
        cos = cos_ref[...]
        sin = sin_ref[...]
        parts = []
        for h in range(tn // dk):
            x1 = acc[:, h * dk:h * dk + half]
            x2 = acc[:, h * dk + half:(h + 1) * dk]
            parts.append((x1 * cos - x2 * sin) * scale)
            parts.append((x1 * sin + x2 * cos) * scale)
        return jnp.concatenate(parts, axis=-1)

    @pl.when(n == 0)
    def _():
        o_ref[...] = rotate(1.0).astype(BF16)

    @pl.when(n == 1)
    def _():
        o_ref[...] = rotate(dk ** -0.5).astype(BF16)

    @pl.when(n > 1)
    def _():
        o_ref[...] = acc.astype(BF16)


def _rt_proj(h1, w_in, cos, sin, dk):
    b, lt, d = h1.shape
    n_out = w_in.shape[1]
    tn = d
    t = ROW_TILE
    half = dk // 2
    return pl.pallas_call(
        functools.partial(_rt_proj_kernel, dk=dk),
        out_shape=jax.ShapeDtypeStruct((b, lt, n_out), BF16),
        grid=(n_out // tn, b, lt // t),
        in_specs=[
            pl.BlockSpec((None, t, d), lambda n, bi, j: (bi, j, 0)),
            pl.BlockSpec((d, tn), lambda n, bi, j: (0, n)),
            pl.BlockSpec((t, half), lambda n, bi, j: (j, 0)),
            pl.BlockSpec((t, half), lambda n, bi, j: (j, 0)),
        ],
        out_specs=pl.BlockSpec((None, t, tn), lambda n, bi, j: (bi, j, n)),
        compiler_params=_cparams(("parallel", "parallel", "parallel")),
        name="rt_proj",
    )(h1, w_in, cos, sin)


def _ret_kernel(lg_ref, q_ref, k_ref, v_ref, o_ref, r_ref, *, reverse):
    c = q_ref.shape[0]
    s = pl.program_id(2)

    @pl.when(s == 0)
    def _():
        r_ref[...] = jnp.zeros_like(r_ref)

    x = lg_ref[0]
    lg = jnp.minimum(x, 0.0) - jnp.log1p(jnp.exp(-jnp.abs(x)))
    lg_s = lg[:, 0:1]
    ri = lax.broadcasted_iota(jnp.int32, (c, c), 0)
    ci = lax.broadcasted_iota(jnp.int32, (c, c), 1)
    diff = ((ci - ri) if reverse else (ri - ci)).astype(F32)
    mask = (diff > 0) if reverse else (diff >= 0)
    dm = jnp.where(mask, jnp.exp(jnp.where(mask, diff, 0.0) * lg_s), 0.0)
    pos = lax.broadcasted_iota(jnp.int32, (c, 1), 0).astype(F32)
    p = (c - 1 - pos) if reverse else pos
    dq = jnp.exp((p + 1.0) * lg_s)
    dkey = jnp.exp((c - 1.0 - p) * lg_s)
    dchunk = jnp.exp(float(c) * lg_s)

    q = q_ref[...]
    k = k_ref[...]
    v = v_ref[...]
    r0 = r_ref[...]
    sc = lax.dot_general(q, k, _NT, preferred_element_type=F32) * dm
    o_ref[...] = _mm(sc, v) + _mm(q, r0) * dq
    kd = (k.astype(F32) * dkey).astype(BF16)
    r_ref[...] = r0 * dchunk + lax.dot_general(kd, v, _TN, preferred_element_type=F32)


def _ret(qkvg, decay_logit_dir, direction, n_ctx_blocks, dk, dv):
    b, lt, _ = qkvg.shape
    c = RET_CHUNK
    nblk = lt // c
    nh = RET_HEADS
    reverse = direction == 1
    if reverse:
        def blk(s):
            return jnp.where(s < n_ctx_blocks, n_ctx_blocks - 1 - s, nblk - 1 - (s - n_ctx_blocks))
    else:
        def blk(s):
            return s
    koff = (nh * dk) // dk
    voff = (2 * nh * dk) // dv
    lg = jnp.broadcast_to(decay_logit_dir.astype(F32)[:, None, None], (nh, 1, LANE))
    return pl.pallas_call(
        functools.partial(_ret_kernel, reverse=reverse),
        out_shape=jax.ShapeDtypeStruct((b, lt, nh * dv), F32),
        grid=(b, nh, nblk),
        in_specs=[
            pl.BlockSpec((1, 1, LANE), lambda bi, h, s: (h, 0, 0)),
            pl.BlockSpec((None, c, dk), lambda bi, h, s: (bi, blk(s), h)),
            pl.BlockSpec((None, c, dk), lambda bi, h, s: (bi, blk(s), koff + h)),
            pl.BlockSpec((None, c, dv), lambda bi, h, s: (bi, blk(s), voff + h)),
        ],
        out_specs=pl.BlockSpec((None, c, dv), lambda bi, h, s: (bi, blk(s), h)),
        scratch_shapes=[pltpu.VMEM((dk, dv), F32)],
        compiler_params=_cparams(("parallel", "parallel", "arbitrary")),
        name="ret_bwd" if reverse else "ret_fwd",
    )(lg, qkvg, qkvg, qkvg)


def _rt_out_kernel(of_ref, ob_ref, g_ref, x_ref, mod_ref, gng_ref, fg_ref, w_ref, o_ref, *, dv):
    o = of_ref[...] + ob_ref[...]
    parts = []
    for h in range(o.shape[-1] // dv):
        oh = o[:, h * dv:(h + 1) * dv]
        parts.append(oh * lax.rsqrt(jnp.mean(oh * oh, axis=-1, keepdims=True) + EPS))
    on = jnp.concatenate(parts, axis=-1)
    y = on * gng_ref[...] * _silu(g_ref[...].astype(F32))
    new = x_ref[...] + mod_ref[2:3, :] * _mm(y, w_ref[...])
    o_ref[...] = new * lax.rsqrt(jnp.mean(new * new, axis=-1, keepdims=True) + EPS) * fg_ref[...]


def _rt_out(o_f, o_b, qkvg, x1, mod1, gn_g, final_g, w_out, n_ctx_tiles, dv):
    b, l, d = x1.shape
    vd = w_out.shape[0]
    t = OUT_TILE
    goff = (qkvg.shape[-1] - vd) // vd
    return pl.pallas_call(
        functools.partial(_rt_out_kernel, dv=dv),
        out_shape=jax.ShapeDtypeStruct((b, l, d), F32),
        grid=(b, l // t),
        in_specs=[
            pl.BlockSpec((None, t, vd), lambda bi, j: (bi, j + n_ctx_tiles, 0)),
            pl.BlockSpec((None, t, vd), lambda bi, j: (bi, j + n_ctx_tiles, 0)),
            pl.BlockSpec((None, t, vd), lambda bi, j: (bi, j + n_ctx_tiles, goff)),
            pl.BlockSpec((None, t, d), lambda bi, j: (bi, j, 0)),
            pl.BlockSpec((None, 3, d), lambda bi, j: (bi, 0, 0)),
            pl.BlockSpec((1, vd), lambda bi, j: (0, 0)),
            pl.BlockSpec((1, d), lambda bi, j: (0, 0)),
            pl.BlockSpec((vd, d), lambda bi, j: (0, 0)),
        ],
        out_specs=pl.BlockSpec((None, t, d), lambda bi, j: (bi, j, 0)),
        compiler_params=_cparams(("parallel", "parallel")),
        name="rt_out",
    )(o_f, o_b, qkvg, x1, mod1, gn_g.reshape(1, vd), final_g.reshape(1, d), w_out)


def _rope_tables(l, lc, dk):
    t = jnp.arange(l)
    row = (t // GRID_W).astype(F32)
    col = (t % GRID_W).astype(F32)
    nf = dk // 4
    inv = ROPE_BASE ** (-jnp.arange(nf, dtype=F32) / nf)
    ang = jnp.concatenate([row[:, None] * inv, col[:, None] * inv], axis=-1)
    cos = jnp.concatenate([jnp.ones((lc, dk // 2), F32), jnp.cos(ang)], axis=0)
    sin = jnp.concatenate([jnp.zeros((lc, dk // 2), F32), jnp.sin(ang)], axis=0)
    return cos, sin


def kernel(x, c, ctx, c_ctx, ada_w, ada_b, norm_g, rk_mix, rk_w_in, rk_w0, rk_w1, rk_w2, rk_a0, rk_a1,
           rk_a2, rk_k_k, rk_k_a, rk_r_k, rk_ln_g, rk_ln_b, rk_w_out, rt_w_in, rt_decay_logit, rt_gn_g,
           rt_w_out, final_g):
    b, l, d = x.shape
    lc = ctx.shape[1]
    lt = lc + l
    assert lc == ROW_TILE and l % ROW_TILE == 0 and ROW_TILE % GRID_W == 0
    assert d % min(WKV_LANES, d) == 0 and d % WKV_GROUP == 0 and (b * lt) % MM_TILE == 0
    assert ada_w.shape[0] == 2 and rk_w_in.shape[0] == 1 and rt_w_in.shape[0] == 1
    dk = d // RET_HEADS
    dv = rt_w_out.shape[1] // RET_HEADS

    n_mod = -(-(b + 1) // 8) * 8
    cv = jnp.concatenate([c, c_ctx[None], jnp.zeros((n_mod - b - 1, d), F32)], axis=0)
    mod = _adaln(cv, ada_w, ada_b).reshape(2, n_mod, 3, d)

    xm = _rk_pro(x, ctx, mod[0], norm_g[0], rk_mix[0])
    xm2 = xm.reshape(6, b * lt, d)
    proj = _rk_proj(xm2, rk_w_in[0].astype(BF16)).reshape(4, b, lt, d)
    lw, a = _rk_lora(xm2, rk_w1[0].astype(BF16), rk_w2[0].astype(BF16), rk_a1[0].astype(BF16),
                     rk_a2[0].astype(BF16), rk_w0[0], rk_a0[0])
    lw = lw.reshape(2, b, lt, d)
    a = a.reshape(2, b, lt, d)
    k_k = rk_k_k[0].reshape(1, d)
    k_a = rk_k_a[0].reshape(1, d)
    n_ctx_blocks = lc // ROW_TILE
    o_f = _wkv(proj, lw, a, k_k, k_a, 0, n_ctx_blocks)
    o_b = _wkv(proj, lw, a, k_k, k_a, 1, n_ctx_blocks)
    x1, ctx1, h1 = _rk_out(o_f, o_b, proj, a, x, ctx, mod[0], mod[1], norm_g[1], k_a,
                           rk_r_k[0].reshape(1, d), rk_ln_g[0].reshape(1, d), rk_ln_b[0].reshape(1, d),
                           rk_w_out[0].astype(BF16))
    del ctx1

    cos, sin = _rope_tables(l, lc, dk)
    qkvg = _rt_proj(h1, rt_w_in[0].astype(BF16), cos, sin, dk)
    r_f = _ret(qkvg, rt_decay_logit[0, 0], 0, lc // RET_CHUNK, dk, dv)
    r_b = _ret(qkvg, rt_decay_logit[0, 1], 1, lc // RET_CHUNK, dk, dv)
    return _rt_out(r_f, r_b, qkvg, x1, mod[1], rt_gn_g[0], final_g, rt_w_out[0].astype(BF16),
                   lc // OUT_TILE, dv)
```

```python
import functools
import math

import jax
import jax.numpy as jnp
from jax import lax
from jax.experimental import pallas as pl
from jax.experimental.pallas import tpu as pltpu

F32 = jnp.float32
BF16 = jnp.bfloat16
HIGHEST = lax.Precision.HIGHEST

GRID_W = 64
EPS = 1e-6
RWKV_HEAD = 64
RWKV_GN_EPS = 64e-5
RET_HEADS = 8
ROPE_BASE = 10000.0

LANE = 128
MXU_DIM = 256
VMEM_LIMIT = 56 * 1024 * 1024

ROW_TILE = 256
OUT_TILE = 128
MM_TILE = 512
RT_PROJ_TILE = 1024
WKV_CHUNK = 64
WKV_GROUP = MXU_DIM
WKV_LANES = 1024
RET_CHUNK = 256


def _cparams(sem, vmem=VMEM_LIMIT):
    return pltpu.CompilerParams(dimension_semantics=sem, vmem_limit_bytes=vmem)


def _mm(a, b):
    return jnp.dot(a.astype(BF16), b.astype(BF16), preferred_element_type=F32)


def _mm_nt(a, b):
    return lax.dot_general(a.astype(BF16), b.astype(BF16), (((1,), (1,)), ((), ())),
                           preferred_element_type=F32)


def _mm_tn(a, b):
    return lax.dot_general(a.astype(BF16), b.astype(BF16), (((0,), (0,)), ((), ())),
                           preferred_element_type=F32)


def _split2(x):
    hi = x.astype(BF16)
    lo = (x - hi.astype(F32)).astype(BF16)
    return hi, lo


def _split3(x):
    hi = x.astype(BF16)
    r = x - hi.astype(F32)
    mid = r.astype(BF16)
    lo = (r - mid.astype(F32)).astype(BF16)
    return hi, mid, lo


def _dot(a, b, dims):
    return lax.dot_general(a, b, dims, preferred_element_type=F32)


_NN = (((1,), (0,)), ((), ()))
_NT = (((1,), (1,)), ((), ()))
_TN = (((0,), (0,)), ((), ()))


def _mm3(a, b, dims=_NN):
    a1, a2 = _split2(a)
    b1, b2 = _split2(b)
    return _dot(a1, b1, dims) + (_dot(a1, b2, dims) + _dot(a2, b1, dims))


def _mm_exact_lhs(a_bf16, b):
    b1, b2, b3 = _split3(b)
    return _dot(a_bf16, b1, _NN) + (_dot(a_bf16, b2, _NN) + _dot(a_bf16, b3, _NN))


def _mm_exact_rhs(a, b_bf16):
    a1, a2, a3 = _split3(a)
    return _dot(a1, b_bf16, _NN) + (_dot(a2, b_bf16, _NN) + _dot(a3, b_bf16, _NN))


def _silu(x):
    return x * jax.nn.sigmoid(x)


def _norm_mod(xb, g, mod):
    y = xb * lax.rsqrt(jnp.mean(xb * xb, axis=-1, keepdims=True) + EPS) * g
    return y * (1.0 + mod[1:2]) + mod[0:1]


def _adaln_kernel(cv_ref, w_ref, b_ref, o_ref):
    s = _silu(cv_ref[...])
    o_ref[...] = jnp.dot(s, w_ref[...], precision=HIGHEST, preferred_element_type=F32) + b_ref[...]


def _adaln(cv, ada_w, ada_b):
    depth, d, d3 = ada_w.shape
    rows = cv.shape[0]
    tn = math.gcd(d3, 6 * LANE)
    return pl.pallas_call(
        _adaln_kernel,
        out_shape=jax.ShapeDtypeStruct((depth, rows, d3), F32),
        grid=(depth, d3 // tn),
        in_specs=[
            pl.BlockSpec((rows, d), lambda i, n: (0, 0)),
            pl.BlockSpec((None, d, tn), lambda i, n: (i, 0, n)),
            pl.BlockSpec((None, 1, tn), lambda i, n: (i, 0, n)),
        ],
        out_specs=pl.BlockSpec((None, rows, tn), lambda i, n: (i, 0, n)),
        compiler_params=_cparams(("parallel", "parallel")),
        name="adaln",
    )(cv, ada_w, ada_b.reshape(depth, 1, d3))


def _rk_pro_kernel(ctx_ref, x_ref, xp_ref, xn_ref, modc_ref, modl_ref, g_ref, mix_ref, o_ref,
                   *, n_lat_tiles):
    j = pl.program_id(1)
    g = g_ref[...]
    mix = mix_ref[...]
    t, d = x_ref.shape
    row = lax.broadcasted_iota(jnp.int32, (t, 1), 0)

    def emit(h, hs):
        delta = hs - h
        for p in range(mix.shape[0]):
            o_ref[p] = (h + delta * mix[p:p + 1]).astype(BF16)

    @pl.when(j == 0)
    def _():
        h = _norm_mod(ctx_ref[...], g, modc_ref[...])
        half = d // 2
        prev = jnp.where(row == 0, 0.0, pltpu.roll(h[:, :half], 1, 0))
        nxt = jnp.where(row == t - 1, 0.0, pltpu.roll(h[:, half:], t - 1, 0))
        emit(h, jnp.concatenate([prev, nxt], axis=-1))

    @pl.when(j > 0)
    def _():
        mod = modl_ref[...]
        h = _norm_mod(x_ref[...], g, mod)
        hp = _norm_mod(xp_ref[...], g, mod)
        hn = _norm_mod(xn_ref[...], g, mod)
        q = d // 4
        col = row % GRID_W
        left = jnp.where(col == 0, 0.0, pltpu.roll(h[:, :q], 1, 0))
        right = jnp.where(col == GRID_W - 1, 0.0, pltpu.roll(h[:, q:2 * q], t - 1, 0))
        up = jnp.concatenate([hp[:, 2 * q:3 * q], h[:t - GRID_W, 2 * q:3 * q]], axis=0)
        up = jnp.where((j == 1) & (row < GRID_W), 0.0, up)
        down = jnp.concatenate([h[GRID_W:, 3 * q:], hn[:, 3 * q:]], axis=0)
        down = jnp.where((j == n_lat_tiles) & (row >= t - GRID_W), 0.0, down)
        emit(h, jnp.concatenate([left, right, up, down], axis=-1))


def _rk_pro(x, ctx, mod0, norm_g0, mix):
    b, l, d = x.shape
    t = ROW_TILE
    n_lat = l // t
    sub = t // GRID_W
    n_rows = l // GRID_W
    xr = x.reshape(b, n_rows, GRID_W, d)
    nmix = mix.shape[0]
    lt = ctx.shape[1] + l
    return pl.pallas_call(
        functools.partial(_rk_pro_kernel, n_lat_tiles=n_lat),
        out_shape=jax.ShapeDtypeStruct((nmix, b, lt, d), BF16),
        grid=(b, n_lat + 1),
        in_specs=[
            pl.BlockSpec((None, t, d), lambda bi, j: (bi, 0, 0)),
            pl.BlockSpec((None, t, d), lambda bi, j: (bi, jnp.maximum(j - 1, 0), 0)),
            pl.BlockSpec((None, None, GRID_W, d),
                         lambda bi, j: (bi, jnp.maximum((j - 1) * sub - 1, 0), 0, 0)),
            pl.BlockSpec((None, None, GRID_W, d),
                         lambda bi, j: (bi, jnp.minimum(jnp.maximum(j, 1) * sub, n_rows - 1), 0, 0)),
            pl.BlockSpec((None, 3, d), lambda bi, j: (b, 0, 0)),
            pl.BlockSpec((None, 3, d), lambda bi, j: (bi, 0, 0)),
            pl.BlockSpec((1, d), lambda bi, j: (0, 0)),
            pl.BlockSpec((nmix, d), lambda bi, j: (0, 0)),
        ],
        out_specs=pl.BlockSpec((nmix, None, t, d), lambda bi, j: (0, bi, j, 0)),
        compiler_params=_cparams(("parallel", "parallel")),
        name="rk_pro",
    )(ctx, x, xr, xr, mod0, mod0, norm_g0.reshape(1, d), mix)


def _rk_proj_kernel(x_ref, w_ref, o_ref):
    o_ref[...] = jnp.dot(x_ref[...], w_ref[...], preferred_element_type=F32)


def _rk_proj(xm, w_in):
    nproj, d, _ = w_in.shape
    m = xm.shape[1]
    tm = MM_TILE
    return pl.pallas_call(
        _rk_proj_kernel,
        out_shape=jax.ShapeDtypeStruct((nproj, m, d), F32),
        grid=(nproj, m // tm),
        in_specs=[
            pl.BlockSpec((None, tm, d), lambda p, i: (p, i, 0)),
            pl.BlockSpec((None, d, d), lambda p, i: (p, 0, 0)),
        ],
        out_specs=pl.BlockSpec((None, tm, d), lambda p, i: (p, i, 0)),
        compiler_params=_cparams(("parallel", "parallel")),
        name="rk_proj",
    )(xm, w_in)


def _rk_lora_kernel(xw_ref, xa_ref, w1_ref, w2_ref, a1_ref, a2_ref, w0_ref, a0_ref, lw_ref, a_ref):
    xw = xw_ref[...]
    xa = xa_ref[...]
    for n in range(2):
        hw = jnp.tanh(jnp.dot(xw, w1_ref[n], preferred_element_type=F32))
        ww = w0_ref[n:n + 1, :] + _mm(hw, w2_ref[n])
        lw_ref[n] = -math.exp(-0.5) * jax.nn.sigmoid(ww)
        ha = jnp.dot(xa, a1_ref[n], preferred_element_type=F32)
        a_ref[n] = jax.nn.sigmoid(a0_ref[n:n + 1, :] + _mm(ha, a2_ref[n]))


def _rk_lora(xm, w1, w2, a1, a2, w0, a0):
    m, d = xm.shape[1:]
    r = w1.shape[-1]
    tm = MM_TILE
    full = lambda shape: pl.BlockSpec(shape, lambda i: (0,) * len(shape))
    out = jax.ShapeDtypeStruct((2, m, d), F32)
    return pl.pallas_call(
        _rk_lora_kernel,
        out_shape=(out, out),
        grid=(m // tm,),
        in_specs=[
            pl.BlockSpec((None, tm, d), lambda i: (4, i, 0)),
            pl.BlockSpec((None, tm, d), lambda i: (5, i, 0)),
            full((2, d, r)), full((2, r, d)), full((2, d, r)), full((2, r, d)),
            full((2, d)), full((2, d)),
        ],
        out_specs=(pl.BlockSpec((2, tm, d), lambda i: (0, i, 0)),
                   pl.BlockSpec((2, tm, d), lambda i: (0, i, 0))),
        compiler_params=_cparams(("parallel",)),
        name="rk_lora",
    )(xm, xm, w1, w2, a1, a2, w0, a0)


def _wkv_kernel(r_ref, k_ref, v_ref, lw_ref, a_ref, kk_ref, ka_ref, o_ref, t_ref,
                *, reverse, chunks_per_block):
    c = WKV_CHUNK
    gw = WKV_GROUP
    nq = r_ref.shape[-1] // gw
    nh = gw // RWKV_HEAD
    hb = gw
    s = pl.program_id(2)

    @pl.when(s == 0)
    def _():
        t_ref[...] = jnp.zeros_like(t_ref)

    ri = lax.broadcasted_iota(jnp.int32, (c, c), 0)
    ci = lax.broadcasted_iota(jnp.int32, (c, c), 1)
    tri = ((ci >= ri) if reverse else (ci <= ri)).astype(BF16)
    rw = lax.broadcasted_iota(jnp.int32, (c, nh * c), 0)
    cw = lax.broadcasted_iota(jnp.int32, (c, nh * c), 1) % c
    strict_w = (cw > rw) if reverse else (cw < rw)
    incl_w = strict_w | (cw == rw)
    eye_w = (cw == rw).astype(F32)
    rs = lax.broadcasted_iota(jnp.int32, (nh * c, nh * c), 0)
    cs = lax.broadcasted_iota(jnp.int32, (nh * c, nh * c), 1)
    bd_mask = ((rs // c) == (cs // c)).astype(BF16)
    rl = lax.broadcasted_iota(jnp.int32, (hb, hb), 0)
    cl = lax.broadcasted_iota(jnp.int32, (hb, hb), 1)
    head_bd = (rl // RWKV_HEAD) == (cl // RWKV_HEAD)
    ones_bd = head_bd.astype(BF16)
    lane_head = lax.broadcasted_iota(jnp.int32, (1, hb), 1) // RWKV_HEAD
    lane_masks = [(lane_head == h).astype(BF16) for h in range(nh)]
    end_row = 0 if reverse else c - 1

    def stack(x):
        xb = x.astype(BF16)
        return jnp.concatenate([xb * m for m in lane_masks], axis=0)

    def block_diag(xw):
        return jnp.concatenate([xw.astype(BF16)] * nh, axis=0) * bd_mask

    def prepass(cidx, q, out):
        rows = pl.ds(cidx * c, c)
        lanes = pl.ds(q * gw, gw)
        r = r_ref[rows, lanes]
        k = k_ref[rows, lanes]
        v = v_ref[rows, lanes]
        lw = lw_ref[rows, lanes]
        a = a_ref[rows, lanes]
        kap = ka_ref[:, lanes]

        lw_hi, lw_lo = _split2(lw)
        b = _dot(tri, lw_hi, _NN) + _dot(tri, lw_lo, _NN)
        kk = k * kk_ref[:, lanes]
        ssq = ssq_blocks[q][cidx * c:(cidx + 1) * c]
        yield
        b_end = b[end_row:end_row + 1, :]
        e_pos = jnp.exp(b)
        e_neg = jnp.exp(-b)
        e_prev = jnp.exp(b - lw)
        e_end = jnp.exp(b_end - b)
        p_end = jnp.exp(b_end)

        kappa = kk / jnp.maximum(jnp.sqrt(ssq), 1e-12)
        kd = k * (1.0 + (a - 1.0) * kap)
        beta = kappa * a

        kt = kappa * e_prev
        rt = r * e_pos
        kt_s = stack(kt)
        bt_s = stack(beta * e_neg)
        kn_s = stack(kd * e_neg)
        v_s = stack(v)
        bbar = beta * e_end
        kbar = kd * e_end

        lhs = jnp.concatenate([kt, rt], axis=0)
        ab = _mm_nt(lhs, bt_s)
        ak = _mm_nt(lhs, kn_s)
        yield
        a_ab = jnp.where(strict_w, ab[:c], 0.0)
        a_rb = jnp.where(incl_w, ab[c:], 0.0)
        a_ak = jnp.where(strict_w, ak[:c], 0.0)
        a_rk = jnp.where(incl_w, ak[c:], 0.0)

        pw = -a_ab
        minv = eye_w + pw
        pw = _mm(pw, block_diag(pw))
        intra_v = _mm(jnp.concatenate([a_ak, a_rk], axis=0), v_s)
        av = intra_v[:c]
        ark_v = intra_v[c:]
        yield
        n_sq = int(math.log2(c)) - 1
        for i in range(n_sq):
            both = _mm(jnp.concatenate([minv, pw], axis=0) if i < n_sq - 1 else minv, block_diag(pw))
            minv = minv + both[:c]
            pw = both[c:]
            yield

        khat = _mm(minv, kt_s)
        wmat = _mm(minv, stack(av))
        yield
        rhat = rt - _mm(a_rb, stack(khat))
        o_intra = ark_v - _mm(a_rb, stack(wmat))
        readout = jnp.concatenate([rhat, khat], axis=0).astype(BF16)
        keys = jnp.concatenate([kbar, bbar], axis=0).astype(BF16)
        out[(cidx, q)] = (readout, o_intra, wmat, v.astype(BF16), keys, p_end)

    order = list(range(chunks_per_block))
    if reverse:
        order.reverse()
    factors = {}
    ssq_blocks = []
    for q in range(nq):
        kk_blk = k_ref[:, pl.ds(q * gw, gw)] * kk_ref[:, pl.ds(q * gw, gw)]
        ssq_blocks.append(_mm(kk_blk * kk_blk, ones_bd))
    pending = [prepass(cidx, q, factors) for cidx in order for q in range(nq)]
    while pending:
        pending = [gen for gen in pending if next(gen, "done") != "done"]
    for cidx in order:
        rows = pl.ds(cidx * c, c)
        reads = []
        for q in range(nq):
            readout = factors[(cidx, q)][0]
            reads.append(_dot(readout, t_ref[q].astype(BF16), _NT))
        for q in range(nq):
            _, o_intra, wmat, vb, keys, p_end = factors[(cidx, q)]
            o_ref[rows, pl.ds(q * gw, gw)] = reads[q][:c] + o_intra
            u = -(reads[q][c:] + wmat)
            vu = jnp.concatenate([vb, u.astype(BF16)], axis=0)
            t_ref[q] = t_ref[q] * p_end + jnp.where(head_bd, _dot(vu, keys, _TN), 0.0)


def _wkv(proj, lw, a, k_k, k_a, direction, n_ctx_blocks):
    _, b, lt, d = proj.shape
    rb = ROW_TILE
    hb = min(WKV_LANES, d)
    nblk = lt // rb
    reverse = direction == 1

    if reverse:
        def blk(s):
            return jnp.where(s < n_ctx_blocks, n_ctx_blocks - 1 - s, nblk - 1 - (s - n_ctx_blocks))
    else:
        def blk(s):
            return s

    def tok(p):
        return pl.BlockSpec((None, None, rb, hb), lambda bi, hi, s: (p, bi, blk(s), hi))

    return pl.pallas_call(
        functools.partial(_wkv_kernel, reverse=reverse, chunks_per_block=rb // WKV_CHUNK),
        out_shape=jax.ShapeDtypeStruct((b, lt, d), F32),
        grid=(b, d // hb, nblk),
        in_specs=[tok(0), tok(1), tok(2), tok(direction), tok(direction),
                  pl.BlockSpec((1, hb), lambda bi, hi, s: (0, hi)),
                  pl.BlockSpec((1, hb), lambda bi, hi, s: (0, hi))],
        out_specs=pl.BlockSpec((None, rb, hb), lambda bi, hi, s: (bi, blk(s), hi)),
        scratch_shapes=[pltpu.VMEM((hb // WKV_GROUP, WKV_GROUP, WKV_GROUP), F32)],
        compiler_params=_cparams(("parallel", "parallel", "arbitrary")),
        name="wkv_bwd" if reverse else "wkv_fwd",
    )(proj, proj, proj, lw, a, k_k, k_a)


def _head_sums(x, ones_bd):
    w = ones_bd.shape[0]
    return jnp.concatenate(
        [_mm(x[:, i:i + w], ones_bd) for i in range(0, x.shape[-1], w)], axis=-1)


def _rk_out_kernel(of_ref, ob_ref, r_ref, k_ref, v_ref, g_ref, a_ref, x_ref, ctx_ref,
                   modl_ref, modc_ref, modl1_ref, modc1_ref, ng1_ref,
                   ka_ref, rk_ref, lng_ref, lnb_ref, w_ref,
                   x1_ref, ctx1_ref, h1_ref, *, n_ctx_tiles):
    j = pl.program_id(1)
    hw = min(MXU_DIM, r_ref.shape[-1])
    rl = lax.broadcasted_iota(jnp.int32, (hw, hw), 0)
    cl = lax.broadcasted_iota(jnp.int32, (hw, hw), 1)
    ones_bd = ((rl // RWKV_HEAD) == (cl // RWKV_HEAD)).astype(BF16)
    inv_n = 1.0 / RWKV_HEAD

    groups = [pl.ds(i, hw) for i in range(0, r_ref.shape[-1], hw)]
    o = [of_ref[:, gs] + ob_ref[:, gs] for gs in groups]
    mu = [_mm(og, ones_bd) * inv_n for og in o]
    rkk = []
    for gs in groups:
        ksum = k_ref[:, gs] * (2.0 + (a_ref[0, :, gs] + a_ref[1, :, gs] - 2.0) * ka_ref[:, gs])
        rkk.append(_mm(r_ref[:, gs] * ksum * rk_ref[:, gs], ones_bd))
    oc, var = [], []

    def centre(gi):
        if gi < len(groups):
            oc.append(o[gi] - mu[gi])
            var.append(_mm(oc[gi] * oc[gi], ones_bd) * inv_n)

    centre(0)
    centre(1)
    out = None
    for gi, gs in enumerate(groups):
        centre(gi + 2)
        on = oc[gi] * lax.rsqrt(var[gi] + RWKV_GN_EPS) * lng_ref[:, gs] + lnb_ref[:, gs]
        y = (on + rkk[gi] * v_ref[:, gs]) * _silu(g_ref[:, gs])
        part = _mm(y, w_ref[gs, :])
        out = part if out is None else out + part
    ng1 = ng1_ref[...]

    @pl.when(j < n_ctx_tiles)
    def _():
        rows = pl.ds(pl.multiple_of(j * OUT_TILE, OUT_TILE), OUT_TILE)
        new = ctx_ref[rows, :] + modc_ref[2:3, :] * out
        ctx1_ref[rows, :] = new
        h1_ref[...] = _norm_mod(new, ng1, modc1_ref[...]).astype(BF16)

    @pl.when(j >= n_ctx_tiles)
    def _():
        new = x_ref[...] + modl_ref[2:3, :] * out
        x1_ref[...] = new
        h1_ref[...] = _norm_mod(new, ng1, modl1_ref[...]).astype(BF16)


def _rk_out(o_f, o_b, proj, a, x, ctx, mod0, mod1, norm_g1, k_a, r_k, ln_g, ln_b, w_out):
    b, l, d = x.shape
    lc = ctx.shape[1]
    lt = lc + l
    t = OUT_TILE
    nct = lc // t
    tok = pl.BlockSpec((None, t, d), lambda bi, j: (bi, j, 0))
    ptok = lambda p: pl.BlockSpec((None, None, t, d), lambda bi, j: (p, bi, j, 0))
    vec = pl.BlockSpec((1, d), lambda bi, j: (0, 0))
    lat_blk = lambda bi, j: (bi, jnp.maximum(j - nct, 0), 0)
    mod_l = pl.BlockSpec((None, 3, d), lambda bi, j: (bi, 0, 0))
    mod_c = pl.BlockSpec((None, 3, d), lambda bi, j: (b, 0, 0))
    return pl.pallas_call(
        functools.partial(_rk_out_kernel, n_ctx_tiles=nct),
        out_shape=(jax.ShapeDtypeStruct((b, l, d), F32),
                   jax.ShapeDtypeStruct((b, lc, d), F32),
                   jax.ShapeDtypeStruct((b, lt, d), BF16)),
        grid=(b, lt // t),
        in_specs=[tok, tok, ptok(0), ptok(1), ptok(2), ptok(3),
                  pl.BlockSpec((2, None, t, d), lambda bi, j: (0, bi, j, 0)),
                  pl.BlockSpec((None, t, d), lat_blk),
                  pl.BlockSpec((None, lc, d), lambda bi, j: (bi, 0, 0)),
                  mod_l, mod_c, mod_l, mod_c, vec, vec, vec, vec, vec,
                  pl.BlockSpec((d, d), lambda bi, j: (0, 0))],
        out_specs=(pl.BlockSpec((None, t, d), lat_blk),
                   pl.BlockSpec((None, lc, d), lambda bi, j: (bi, 0, 0)),
                   pl.BlockSpec((None, t, d), lambda bi, j: (bi, j, 0))),
        compiler_params=_cparams(("parallel", "arbitrary")),
        name="rk_out",
    )(o_f, o_b, proj, proj, proj, proj, a, x, ctx, mod0, mod0, mod1, mod1,
      norm_g1.reshape(1, d), k_a, r_k, ln_g, ln_b, w_out)


def _rt_proj_kernel(h_ref, w_ref, cos_ref, sin_ref, o_ref, *, dk):
    n = pl.program_id(0)
    tn = w_ref.shape[-1]
    half = dk // 2
    sub = min(ROW_TILE, h_ref.shape[0])
    chunks = [pl.ds(i, sub) for i in range(0, h_ref.shape[0], sub)]

    def project(rows):
        return jnp.dot(h_ref[rows, :], w_ref[...], preferred_element_type=F32)

    def rotate(scale):
        for rows in chunks:
            acc = project(rows)
            cos = cos_ref[rows, :]
            sin = sin_ref[rows, :]
            parts = []
            for h in range(tn // dk):
                x1 = acc[:, h * dk:h * dk + half]
                x2 = acc[:, h * dk + half:(h + 1) * dk]
                parts.append((x1 * cos - x2 * sin) * scale)
                parts.append((x1 * sin + x2 * cos) * scale)
            o_ref[rows, :] = jnp.concatenate(parts, axis=-1).astype(BF16)

    @pl.when(n == 0)
    def _():
        rotate(1.0)

    @pl.when(n == 1)
    def _():
        rotate(dk ** -0.5)

    @pl.when(n > 1)
    def _():
        for rows in chunks:
            o_ref[rows, :] = project(rows).astype(BF16)


def _rt_proj(h1, w_in, cos, sin, dk):
    m, d = h1.shape
    n_out = w_in.shape[1]
    tn = d
    tm = RT_PROJ_TILE if m % RT_PROJ_TILE == 0 else MM_TILE
    half = dk // 2
    return pl.pallas_call(
        functools.partial(_rt_proj_kernel, dk=dk),
        out_shape=jax.ShapeDtypeStruct((m, n_out), BF16),
        grid=(n_out // tn, m // tm),
        in_specs=[
            pl.BlockSpec((tm, d), lambda n, i: (i, 0)),
            pl.BlockSpec((d, tn), lambda n, i: (0, n)),
            pl.BlockSpec((tm, half), lambda n, i: (i, 0)),
            pl.BlockSpec((tm, half), lambda n, i: (i, 0)),
        ],
        out_specs=pl.BlockSpec((tm, tn), lambda n, i: (i, n)),
        compiler_params=_cparams(("parallel", "parallel")),
        name="rt_proj",
    )(h1, w_in, cos, sin)


def _log_sigmoid(x):
    return jnp.minimum(x, 0.0) - jnp.log1p(jnp.exp(-jnp.abs(x)))


def _ret_bstate_kernel(lg_ref, k_ref, v_ref, rb_ref, r_ref, *, dk, dv):
    c = k_ref.shape[0]
    s = pl.program_id(1)

    @pl.when(s == 0)
    def _():
        r_ref[...] = jnp.zeros_like(r_ref)

    pos = lax.broadcasted_iota(jnp.int32, (c, 1), 0).astype(F32)
    for h in range(RET_HEADS):
        lgb = _log_sigmoid(lg_ref[1, h])[:, 0:1]
        k = k_ref[:, h * dk:(h + 1) * dk]
        v = v_ref[:, h * dv:(h + 1) * dv]
        r0 = r_ref[h]
        rb_ref[h] = r0.astype(BF16)
        kd = (k.astype(F32) * jnp.exp(pos * lgb)).astype(BF16)
        r_ref[h] = r0 * jnp.exp(float(c) * lgb) + _dot(kd, v, _TN)


def _ret_main_kernel(lg_ref, q_ref, k_ref, v_ref, rb_ref, gng_ref, y_ref, rf_ref, dm_ref, dvec_ref,
                     *, dk, dv, n_ctx_blocks):
    c = q_ref.shape[0]
    s = pl.program_id(1)

    @pl.when(s == 0)
    def _():
        rf_ref[...] = jnp.zeros_like(rf_ref)
        diff = (lax.broadcasted_iota(jnp.int32, (c, c), 0)
                - lax.broadcasted_iota(jnp.int32, (c, c), 1)).astype(F32)
        pos = lax.broadcasted_iota(jnp.int32, (c, LANE), 0).astype(F32)
        for h in range(RET_HEADS):
            lgf = _log_sigmoid(lg_ref[0, h])
            lgb = _log_sigmoid(lg_ref[1, h])
            dm_ref[h] = jnp.where(diff >= 0, jnp.exp(jnp.maximum(diff, 0.0) * lgf[:, 0:1]),
                                  jnp.exp(jnp.maximum(-diff, 0.0) * lgb[:, 0:1]))
            dvec_ref[h, 0] = jnp.exp((pos + 1.0) * lgf)
            dvec_ref[h, 1] = jnp.exp((float(c) - pos) * lgb)
            dvec_ref[h, 2] = jnp.exp((float(c) - 1.0 - pos) * lgf)

    def lanes(x, width):
        return x[:, :width] if width <= LANE else jnp.concatenate([x] * (width // LANE), axis=1)

    @pl.when(s >= n_ctx_blocks)
    def _():
        scores = []
        for h in range(RET_HEADS):
            q = q_ref[:, h * dk:(h + 1) * dk]
            k = k_ref[:, h * dk:(h + 1) * dk]
            scores.append(_dot(q, k, _NT))
        outs = []
        for h in range(RET_HEADS):
            q = q_ref[:, h * dk:(h + 1) * dk]
            sc = (scores[h] * dm_ref[h]).astype(BF16)
            lhs = jnp.concatenate([sc, q * lanes(dvec_ref[h, 0], dk).astype(BF16),
                                   q * lanes(dvec_ref[h, 1], dk).astype(BF16)], axis=1)
            rhs = jnp.concatenate([v_ref[:, h * dv:(h + 1) * dv], rf_ref[h].astype(BF16), rb_ref[h]],
                                  axis=0)
            outs.append(_dot(lhs, rhs, _NN))
        for h in range(RET_HEADS):
            o = outs[h]
            on = o * lax.rsqrt(jnp.mean(o * o, axis=-1, keepdims=True) + EPS)
            y_ref[:, h * dv:(h + 1) * dv] = (on * gng_ref[:, h * dv:(h + 1) * dv]).astype(BF16)

    for h in range(RET_HEADS):
        k = k_ref[:, h * dk:(h + 1) * dk]
        v = v_ref[:, h * dv:(h + 1) * dv]
        kd = (k.astype(F32) * lanes(dvec_ref[h, 2], dk)).astype(BF16)
        gamma_c = jnp.exp(float(c) * _log_sigmoid(lg_ref[0, h]))
        rf_ref[h] = rf_ref[h] * lanes(gamma_c, dv) + _dot(kd, v, _TN)


def _ret(qkvg, decay_logit, gn_g, n_ctx_blocks, dk, dv):
    b, lt, _ = qkvg.shape
    c = RET_CHUNK
    nblk = lt // c
    nh = RET_HEADS
    d = nh * dk
    vd = nh * dv
    assert (2 * d) % vd == 0
    voff = (2 * d) // vd
    lg = jnp.broadcast_to(decay_logit.astype(F32)[:, :, None, None], (2, nh, 1, LANE))
    lg_spec = pl.BlockSpec((2, nh, 1, LANE), lambda bi, s: (0, 0, 0, 0))

    def bwd_blk(s):
        return jnp.where(s < n_ctx_blocks, n_ctx_blocks - 1 - s, nblk - 1 - (s - n_ctx_blocks))

    r_b = pl.pallas_call(
        functools.partial(_ret_bstate_kernel, dk=dk, dv=dv),
        out_shape=jax.ShapeDtypeStruct((b, nblk, nh, dk, dv), BF16),
        grid=(b, nblk),
        in_specs=[
            lg_spec,
            pl.BlockSpec((None, c, d), lambda bi, s: (bi, bwd_blk(s), 1)),
            pl.BlockSpec((None, c, vd), lambda bi, s: (bi, bwd_blk(s), voff)),
        ],
        out_specs=pl.BlockSpec((None, None, nh, dk, dv), lambda bi, s: (bi, bwd_blk(s), 0, 0, 0)),
        scratch_shapes=[pltpu.VMEM((nh, dk, dv), F32)],
        compiler_params=_cparams(("parallel", "arbitrary")),
        name="ret_bstate",
    )(lg, qkvg, qkvg)

    lat = lambda bi, s: (bi, jnp.maximum(s - n_ctx_blocks, 0), 0)
    return pl.pallas_call(
        functools.partial(_ret_main_kernel, dk=dk, dv=dv, n_ctx_blocks=n_ctx_blocks),
        out_shape=jax.ShapeDtypeStruct((b, lt - n_ctx_blocks * c, vd), BF16),
        grid=(b, nblk),
        in_specs=[
            lg_spec,
            pl.BlockSpec((None, c, d), lambda bi, s: (bi, s, 0)),
            pl.BlockSpec((None, c, d), lambda bi, s: (bi, s, 1)),
            pl.BlockSpec((None, c, vd), lambda bi, s: (bi, s, voff)),
            pl.BlockSpec((None, None, nh, dk, dv), lambda bi, s: (bi, s, 0, 0, 0)),
            pl.BlockSpec((1, vd), lambda bi, s: (0, 0)),
        ],
        out_specs=pl.BlockSpec((None, c, vd), lat),
        scratch_shapes=[pltpu.VMEM((nh, dk, dv), F32), pltpu.VMEM((nh, c, c), F32),
                        pltpu.VMEM((nh, 3, c, LANE), F32)],
        compiler_params=_cparams(("parallel", "arbitrary")),
        name="ret_main",
    )(lg, qkvg, qkvg, qkvg, r_b, gn_g.reshape(1, vd))


def _rt_out_kernel(y_ref, g_ref, x_ref, mod_ref, fg_ref, w_ref, o_ref):
    gated = y_ref[...].astype(F32) * _silu(g_ref[...].astype(F32))
    new = x_ref[...] + mod_ref[2:3, :] * _mm(gated, w_ref[...])
    o_ref[...] = new * lax.rsqrt(jnp.mean(new * new, axis=-1, keepdims=True) + EPS) * fg_ref[...]


def _rt_out(y, qkvg, x1, mod1, final_g, w_out, n_ctx_tiles):
    b, l, d = x1.shape
    vd = w_out.shape[0]
    t = ROW_TILE
    goff = (qkvg.shape[-1] - vd) // vd
    return pl.pallas_call(
        _rt_out_kernel,
        out_shape=jax.ShapeDtypeStruct((b, l, d), F32),
        grid=(b, l // t),
        in_specs=[
            pl.BlockSpec((None, t, vd), lambda bi, j: (bi, j, 0)),
            pl.BlockSpec((None, t, vd), lambda bi, j: (bi, j + n_ctx_tiles, goff)),
            pl.BlockSpec((None, t, d), lambda bi, j: (bi, j, 0)),
            pl.BlockSpec((None, 3, d), lambda bi, j: (bi, 0, 0)),
            pl.BlockSpec((1, d), lambda bi, j: (0, 0)),
            pl.BlockSpec((vd, d), lambda bi, j: (0, 0)),
        ],
        out_specs=pl.BlockSpec((None, t, d), lambda bi, j: (bi, j, 0)),
        compiler_params=_cparams(("parallel", "parallel")),
        name="rt_out",
    )(y, qkvg, x1, mod1, final_g.reshape(1, d), w_out)


def _rope_tables(l, lc, dk):
    t = jnp.arange(l)
    row = (t // GRID_W).astype(F32)
    col = (t % GRID_W).astype(F32)
    nf = dk // 4
    inv = ROPE_BASE ** (-jnp.arange(nf, dtype=F32) / nf)
    ang = jnp.concatenate([row[:, None] * inv, col[:, None] * inv], axis=-1)
    cos = jnp.concatenate([jnp.ones((lc, dk // 2), F32), jnp.cos(ang)], axis=0)
    sin = jnp.concatenate([jnp.zeros((lc, dk // 2), F32), jnp.sin(ang)], axis=0)
    return cos, sin


def kernel(x, c, ctx, c_ctx, ada_w, ada_b, norm_g, rk_mix, rk_w_in, rk_w0, rk_w1, rk_w2, rk_a0, rk_a1,
           rk_a2, rk_k_k, rk_k_a, rk_r_k, rk_ln_g, rk_ln_b, rk_w_out, rt_w_in, rt_decay_logit, rt_gn_g,
           rt_w_out, final_g):
    b, l, d = x.shape
    lc = ctx.shape[1]
    lt = lc + l
    assert lc == ROW_TILE and l % ROW_TILE == 0 and ROW_TILE % GRID_W == 0
    assert d % min(WKV_LANES, d) == 0 and d % WKV_GROUP == 0 and (b * lt) % MM_TILE == 0
    assert ada_w.shape[0] == 2 and rk_w_in.shape[0] == 1 and rt_w_in.shape[0] == 1
    dk = d // RET_HEADS
    dv = rt_w_out.shape[1] // RET_HEADS

    n_mod = -(-(b + 1) // 8) * 8
    cv = jnp.concatenate([c, c_ctx[None], jnp.zeros((n_mod - b - 1, d), F32)], axis=0)
    mod = _adaln(cv, ada_w, ada_b).reshape(2, n_mod, 3, d)

    xm = _rk_pro(x, ctx, mod[0], norm_g[0], rk_mix[0])
    xm2 = xm.reshape(6, b * lt, d)
    proj = _rk_proj(xm2, rk_w_in[0].astype(BF16)).reshape(4, b, lt, d)
    lw, a = _rk_lora(xm2, rk_w1[0].astype(BF16), rk_w2[0].astype(BF16), rk_a1[0].astype(BF16),
                     rk_a2[0].astype(BF16), rk_w0[0], rk_a0[0])
    lw = lw.reshape(2, b, lt, d)
    a = a.reshape(2, b, lt, d)
    k_k = rk_k_k[0].reshape(1, d)
    k_a = rk_k_a[0].reshape(1, d)
    n_ctx_blocks = lc // ROW_TILE
    o_f = _wkv(proj, lw, a, k_k, k_a, 0, n_ctx_blocks)
    o_b = _wkv(proj, lw, a, k_k, k_a, 1, n_ctx_blocks)
    x1, ctx1, h1 = _rk_out(o_f, o_b, proj, a, x, ctx, mod[0], mod[1], norm_g[1], k_a,
                           rk_r_k[0].reshape(1, d), rk_ln_g[0].reshape(1, d), rk_ln_b[0].reshape(1, d),
                           rk_w_out[0].astype(BF16))
    del ctx1

    cos, sin = _rope_tables(l, lc, dk)
    qkvg = _rt_proj(h1.reshape(b * lt, d), rt_w_in[0].astype(BF16), jnp.tile(cos, (b, 1)),
                    jnp.tile(sin, (b, 1)), dk)
    y = _ret(qkvg.reshape(b, lt, -1), rt_decay_logit[0], rt_gn_g[0], lc // RET_CHUNK, dk, dv)
    return _rt_out(y, qkvg.reshape(b, lt, -1), x1, mod[1], final_g, rt_w_out[0].astype(BF16),
                   lc // ROW_TILE)
```

```python
import functools
import math

import jax
import jax.numpy as jnp
from jax import lax
from jax.experimental import pallas as pl
from jax.experimental.pallas import tpu as pltpu

F32 = jnp.float32
BF16 = jnp.bfloat16
HIGHEST = lax.Precision.HIGHEST

GRID_W = 64
EPS = 1e-6
RWKV_HEAD = 64
RWKV_GN_EPS = 64e-5
RET_HEADS = 8
ROPE_BASE = 10000.0

LANE = 128
MXU_DIM = 256
VMEM_LIMIT = 56 * 1024 * 1024

ROW_TILE = 256
OUT_TILE = 128
MM_TILE = 512
RT_PROJ_TILE = 1024
WKV_CHUNK = 64
WKV_GROUP = MXU_DIM
WKV_STAGGER = 3
WKV_LANES = 2048
RET_CHUNK = 256
ACT_DTYPE = BF16


def _cparams(sem, vmem=VMEM_LIMIT):
    return pltpu.CompilerParams(dimension_semantics=sem, vmem_limit_bytes=vmem)


def _mm(a, b):
    return jnp.dot(a.astype(BF16), b.astype(BF16), preferred_element_type=F32)


def _mm_nt(a, b):
    return lax.dot_general(a.astype(BF16), b.astype(BF16), (((1,), (1,)), ((), ())),
                           preferred_element_type=F32)


def _mm_tn(a, b):
    return lax.dot_general(a.astype(BF16), b.astype(BF16), (((0,), (0,)), ((), ())),
                           preferred_element_type=F32)


def _split2(x):
    hi = x.astype(BF16)
    lo = (x - hi.astype(F32)).astype(BF16)
    return hi, lo


def _split3(x):
    hi = x.astype(BF16)
    r = x - hi.astype(F32)
    mid = r.astype(BF16)
    lo = (r - mid.astype(F32)).astype(BF16)
    return hi, mid, lo


def _dot(a, b, dims):
    return lax.dot_general(a, b, dims, preferred_element_type=F32)


_NN = (((1,), (0,)), ((), ()))
_NT = (((1,), (1,)), ((), ()))
_TN = (((0,), (0,)), ((), ()))


def _mm3(a, b, dims=_NN):
    a1, a2 = _split2(a)
    b1, b2 = _split2(b)
    return _dot(a1, b1, dims) + (_dot(a1, b2, dims) + _dot(a2, b1, dims))


def _mm_exact_lhs(a_bf16, b):
    b1, b2, b3 = _split3(b)
    return _dot(a_bf16, b1, _NN) + (_dot(a_bf16, b2, _NN) + _dot(a_bf16, b3, _NN))


def _mm_exact_rhs(a, b_bf16):
    a1, a2, a3 = _split3(a)
    return _dot(a1, b_bf16, _NN) + (_dot(a2, b_bf16, _NN) + _dot(a3, b_bf16, _NN))


def _silu(x):
    return x * jax.nn.sigmoid(x)


def _norm_mod(xb, g, mod):
    y = xb * lax.rsqrt(jnp.mean(xb * xb, axis=-1, keepdims=True) + EPS) * g
    return y * (1.0 + mod[1:2]) + mod[0:1]


def _adaln_kernel(cv_ref, w_ref, b_ref, o_ref):
    s = _silu(cv_ref[...])
    o_ref[...] = jnp.dot(s, w_ref[...], precision=HIGHEST, preferred_element_type=F32) + b_ref[...]


def _adaln(cv, ada_w, ada_b):
    depth, d, d3 = ada_w.shape
    rows = cv.shape[0]
    tn = math.gcd(d3, 6 * LANE)
    return pl.pallas_call(
        _adaln_kernel,
        out_shape=jax.ShapeDtypeStruct((depth, rows, d3), F32),
        grid=(depth, d3 // tn),
        in_specs=[
            pl.BlockSpec((rows, d), lambda i, n: (0, 0)),
            pl.BlockSpec((None, d, tn), lambda i, n: (i, 0, n)),
            pl.BlockSpec((None, 1, tn), lambda i, n: (i, 0, n)),
        ],
        out_specs=pl.BlockSpec((None, rows, tn), lambda i, n: (i, 0, n)),
        compiler_params=_cparams(("parallel", "parallel")),
        name="adaln",
    )(cv, ada_w, ada_b.reshape(depth, 1, d3))


def _rk_pro_kernel(ctx_ref, x_ref, xp_ref, xn_ref, modc_ref, modl_ref, g_ref, mix_ref,
                   w1_ref, w2_ref, a1_ref, a2_ref, w0_ref, a0_ref, o_ref, lw_ref, a_ref,
                   *, n_lat_tiles):
    j = pl.program_id(1)
    g = g_ref[...]
    mix = mix_ref[...]
    t, d = x_ref.shape
    row = lax.broadcasted_iota(jnp.int32, (t, 1), 0)
    n_proj = o_ref.shape[0]

    def emit(h, hs):
        delta = hs - h
        for p in range(n_proj):
            o_ref[p] = (h + delta * mix[p:p + 1]).astype(BF16)
        xw = (h + delta * mix[n_proj:n_proj + 1]).astype(BF16)
        xa = (h + delta * mix[n_proj + 1:n_proj + 2]).astype(BF16)
        for n in range(2):
            hw = jnp.tanh(jnp.dot(xw, w1_ref[n], preferred_element_type=F32))
            ww = w0_ref[n:n + 1, :] + _mm(hw, w2_ref[n])
            lw_ref[n] = -math.exp(-0.5) * jax.nn.sigmoid(ww)
            ha = jnp.dot(xa, a1_ref[n], preferred_element_type=F32)
            a_ref[n] = jax.nn.sigmoid(a0_ref[n:n + 1, :] + _mm(ha, a2_ref[n])).astype(a_ref.dtype)

    @pl.when(j == 0)
    def _():
        h = _norm_mod(ctx_ref[...], g, modc_ref[...])
        half = d // 2
        prev = jnp.where(row == 0, 0.0, pltpu.roll(h[:, :half], 1, 0))
        nxt = jnp.where(row == t - 1, 0.0, pltpu.roll(h[:, half:], t - 1, 0))
        emit(h, jnp.concatenate([prev, nxt], axis=-1))

    @pl.when(j > 0)
    def _():
        mod = modl_ref[...]
        h = _norm_mod(x_ref[...], g, mod)
        hp = _norm_mod(xp_ref[...], g, mod)
        hn = _norm_mod(xn_ref[...], g, mod)
        q = d // 4
        col = row % GRID_W
        left = jnp.where(col == 0, 0.0, pltpu.roll(h[:, :q], 1, 0))
        right = jnp.where(col == GRID_W - 1, 0.0, pltpu.roll(h[:, q:2 * q], t - 1, 0))
        up = jnp.concatenate([hp[:, 2 * q:3 * q], h[:t - GRID_W, 2 * q:3 * q]], axis=0)
        up = jnp.where((j == 1) & (row < GRID_W), 0.0, up)
        down = jnp.concatenate([h[GRID_W:, 3 * q:], hn[:, 3 * q:]], axis=0)
        down = jnp.where((j == n_lat_tiles) & (row >= t - GRID_W), 0.0, down)
        emit(h, jnp.concatenate([left, right, up, down], axis=-1))


def _rk_pro(x, ctx, mod0, norm_g0, mix, w1, w2, a1, a2, w0, a0):
    b, l, d = x.shape
    t = ROW_TILE
    n_lat = l // t
    sub = t // GRID_W
    n_rows = l // GRID_W
    xr = x.reshape(b, n_rows, GRID_W, d)
    nmix = mix.shape[0]
    n_proj = nmix - 2
    r = w1.shape[-1]
    lt = ctx.shape[1] + l
    full = lambda shape: pl.BlockSpec(shape, lambda bi, j: (0,) * len(shape))
    tok2 = pl.BlockSpec((2, None, t, d), lambda bi, j: (0, bi, j, 0))
    return pl.pallas_call(
        functools.partial(_rk_pro_kernel, n_lat_tiles=n_lat),
        out_shape=(jax.ShapeDtypeStruct((n_proj, b, lt, d), BF16),
                   jax.ShapeDtypeStruct((2, b, lt, d), F32),
                   jax.ShapeDtypeStruct((2, b, lt, d), ACT_DTYPE)),
        grid=(b, n_lat + 1),
        in_specs=[
            pl.BlockSpec((None, t, d), lambda bi, j: (bi, 0, 0)),
            pl.BlockSpec((None, t, d), lambda bi, j: (bi, jnp.maximum(j - 1, 0), 0)),
            pl.BlockSpec((None, None, GRID_W, d),
                         lambda bi, j: (bi, jnp.maximum((j - 1) * sub - 1, 0), 0, 0)),
            pl.BlockSpec((None, None, GRID_W, d),
                         lambda bi, j: (bi, jnp.minimum(jnp.maximum(j, 1) * sub, n_rows - 1), 0, 0)),
            pl.BlockSpec((None, 3, d), lambda bi, j: (b, 0, 0)),
            pl.BlockSpec((None, 3, d), lambda bi, j: (bi, 0, 0)),
            pl.BlockSpec((1, d), lambda bi, j: (0, 0)),
            pl.BlockSpec((nmix, d), lambda bi, j: (0, 0)),
            full((2, d, r)), full((2, r, d)), full((2, d, r)), full((2, r, d)),
            full((2, d)), full((2, d)),
        ],
        out_specs=(pl.BlockSpec((n_proj, None, t, d), lambda bi, j: (0, bi, j, 0)), tok2, tok2),
        compiler_params=_cparams(("parallel", "parallel")),
        name="rk_pro",
    )(ctx, x, xr, xr, mod0, mod0, norm_g0.reshape(1, d), mix, w1, w2, a1, a2, w0, a0)


def _rk_proj_kernel(x_ref, w_ref, o_ref):
    o_ref[...] = jnp.dot(x_ref[...], w_ref[...], preferred_element_type=F32).astype(o_ref.dtype)


def _rk_proj(xm, w_in):
    nproj, d, _ = w_in.shape
    m = xm.shape[1]
    tm = RT_PROJ_TILE if m % RT_PROJ_TILE == 0 else MM_TILE
    return pl.pallas_call(
        _rk_proj_kernel,
        out_shape=jax.ShapeDtypeStruct((nproj, m, d), ACT_DTYPE),
        grid=(nproj, m // tm),
        in_specs=[
            pl.BlockSpec((None, tm, d), lambda p, i: (p, i, 0)),
            pl.BlockSpec((None, d, d), lambda p, i: (p, 0, 0)),
        ],
        out_specs=pl.BlockSpec((None, tm, d), lambda p, i: (p, i, 0)),
        compiler_params=_cparams(("parallel", "parallel")),
        name="rk_proj",
    )(xm, w_in)


def _wkv_kernel(r_ref, k_ref, v_ref, lw_ref, a_ref, kk_ref, ka_ref, o_ref, t_ref,
                *, reverse, chunks_per_block):
    c = WKV_CHUNK
    gw = WKV_GROUP
    nq = r_ref.shape[-1] // gw
    nh = gw // RWKV_HEAD
    hb = gw
    s = pl.program_id(2)

    @pl.when(s == 0)
    def _():
        t_ref[...] = jnp.zeros_like(t_ref)

    ri = lax.broadcasted_iota(jnp.int32, (c, c), 0)
    ci = lax.broadcasted_iota(jnp.int32, (c, c), 1)
    tri = ((ci >= ri) if reverse else (ci <= ri)).astype(BF16)
    rw = lax.broadcasted_iota(jnp.int32, (c, nh * c), 0)
    cw = lax.broadcasted_iota(jnp.int32, (c, nh * c), 1) % c
    strict_w = (cw > rw) if reverse else (cw < rw)
    incl_w = strict_w | (cw == rw)
    eye_w = (cw == rw).astype(F32)
    rs = lax.broadcasted_iota(jnp.int32, (nh * c, nh * c), 0)
    cs = lax.broadcasted_iota(jnp.int32, (nh * c, nh * c), 1)
    bd_mask = ((rs // c) == (cs // c)).astype(BF16)
    rl = lax.broadcasted_iota(jnp.int32, (hb, hb), 0)
    cl = lax.broadcasted_iota(jnp.int32, (hb, hb), 1)
    head_bd = (rl // RWKV_HEAD) == (cl // RWKV_HEAD)
    ones_bd = head_bd.astype(BF16)
    lane_head = lax.broadcasted_iota(jnp.int32, (1, hb), 1) // RWKV_HEAD
    lane_masks = [(lane_head == h).astype(BF16) for h in range(nh)]
    end_row = 0 if reverse else c - 1

    def stack(x):
        xb = x.astype(BF16)
        return jnp.concatenate([xb * m for m in lane_masks], axis=0)

    def block_diag(xw):
        return jnp.concatenate([xw.astype(BF16)] * nh, axis=0) * bd_mask

    def prepass(cidx, q, out):
        rows = pl.ds(cidx * c, c)
        lanes = pl.ds(q * gw, gw)
        r = r_ref[rows, lanes].astype(F32)
        k = k_ref[rows, lanes].astype(F32)
        v = v_ref[rows, lanes].astype(F32)
        lw = lw_ref[rows, lanes]
        a = a_ref[rows, lanes].astype(F32)
        kap = ka_ref[:, lanes]

        lw_hi, lw_lo = _split2(lw)
        b = _dot(tri, lw_hi, _NN) + _dot(tri, lw_lo, _NN)
        kk = k * kk_ref[:, lanes]
        ssq = ssq_blocks[q][cidx * c:(cidx + 1) * c]
        yield
        b_end = b[end_row:end_row + 1, :]
        e_pos = jnp.exp(b)
        e_neg = jnp.exp(-b)
        e_prev = jnp.exp(b - lw)
        e_end = jnp.exp(b_end - b)
        p_end = jnp.exp(b_end)

        kappa = kk / jnp.maximum(jnp.sqrt(ssq), 1e-12)
        kd = k * (1.0 + (a - 1.0) * kap)
        beta = kappa * a

        kt = kappa * e_prev
        rt = r * e_pos
        bt_s = stack(beta * e_neg)
        kn_s = stack(kd * e_neg)
        v_s = stack(v)
        bbar = beta * e_end
        kbar = kd * e_end

        lhs = jnp.concatenate([kt, rt], axis=0)
        ab = _mm_nt(lhs, bt_s)
        ak = _mm_nt(lhs, kn_s)
        yield
        a_ab = jnp.where(strict_w, ab[:c], 0.0)
        a_rb = jnp.where(incl_w, ab[c:], 0.0)
        a_ak = jnp.where(strict_w, ak[:c], 0.0)
        a_rk = jnp.where(incl_w, ak[c:], 0.0)

        pw = -a_ab
        minv = eye_w + pw
        pw = _mm(pw, block_diag(pw))
        intra_v = _mm(jnp.concatenate([a_ak, a_rk], axis=0), v_s)
        av = intra_v[:c]
        ark_v = intra_v[c:]
        yield
        n_sq = int(math.log2(c)) - 1
        for i in range(n_sq):
            both = _mm(jnp.concatenate([minv, pw], axis=0) if i < n_sq - 1 else minv, block_diag(pw))
            minv = minv + both[:c]
            pw = both[c:]
            yield

        keys = jnp.concatenate([kbar, bbar], axis=0).astype(BF16)
        out[(cidx, q)] = (lhs.astype(BF16), minv.astype(BF16), a_rb.astype(BF16), av, ark_v,
                          v.astype(BF16), keys, p_end)

    order = list(range(chunks_per_block))
    if reverse:
        order.reverse()
    factors = {}
    ssq_blocks = []
    for q in range(nq):
        kk_blk = k_ref[:, pl.ds(q * gw, gw)].astype(F32) * kk_ref[:, pl.ds(q * gw, gw)]
        ssq_blocks.append(_mm(kk_blk * kk_blk, ones_bd))
    def recurrence():
        for cidx in order:
            while any((cidx, q) not in factors for q in range(nq)):
                yield
            rows = pl.ds(cidx * c, c)
            fac = [factors[(cidx, q)] for q in range(nq)]
            reads = [_dot(f[0], t_ref[q].astype(BF16), _NT) for q, f in enumerate(fac)]
            yield
            us = [-_dot(f[1], stack(reads[q][:c] + f[3]), _NN) for q, f in enumerate(fac)]
            yield
            for q, f in enumerate(fac):
                _, _, a_rb, _, ark_v, vb, keys, p_end = f
                o = reads[q][c:] + ark_v + _dot(a_rb, stack(us[q]), _NN)
                o_ref[rows, pl.ds(q * gw, gw)] = o.astype(o_ref.dtype)
                vu = jnp.concatenate([vb, us[q].astype(BF16)], axis=0)
                t_ref[q] = t_ref[q] * p_end + jnp.where(head_bd, _dot(vu, keys, _TN), 0.0)
            yield

    chains = [prepass(cidx, q, factors) for cidx in order for q in range(nq)]
    start = [(i // nq) * WKV_STAGGER for i in range(len(chains))]
    live = list(range(len(chains)))
    scan = recurrence()
    rnd = 0
    while scan is not None:
        live = [i for i in live if start[i] > rnd or next(chains[i], "done") != "done"]
        if next(scan, "done") == "done":
            scan = None
        rnd += 1


def _wkv(proj, lw, a, k_k, k_a, direction, n_ctx_blocks):
    _, b, lt, d = proj.shape
    rb = ROW_TILE
    hb = min(WKV_LANES, d)
    nblk = lt // rb
    reverse = direction == 1

    if reverse:
        def blk(s):
            return jnp.where(s < n_ctx_blocks, n_ctx_blocks - 1 - s, nblk - 1 - (s - n_ctx_blocks))
    else:
        def blk(s):
            return s

    def tok(p):
        return pl.BlockSpec((None, None, rb, hb), lambda bi, hi, s: (p, bi, blk(s), hi))

    return pl.pallas_call(
        functools.partial(_wkv_kernel, reverse=reverse, chunks_per_block=rb // WKV_CHUNK),
        out_shape=jax.ShapeDtypeStruct((b, lt, d), ACT_DTYPE),
        grid=(b, d // hb, nblk),
        in_specs=[tok(0), tok(1), tok(2), tok(direction), tok(direction),
                  pl.BlockSpec((1, hb), lambda bi, hi, s: (0, hi)),
                  pl.BlockSpec((1, hb), lambda bi, hi, s: (0, hi))],
        out_specs=pl.BlockSpec((None, rb, hb), lambda bi, hi, s: (bi, blk(s), hi)),
        scratch_shapes=[pltpu.VMEM((hb // WKV_GROUP, WKV_GROUP, WKV_GROUP), F32)],
        compiler_params=_cparams(("parallel", "parallel", "arbitrary")),
        name="wkv_bwd" if reverse else "wkv_fwd",
    )(proj, proj, proj, lw, a, k_k, k_a)


def _head_sums(x, ones_bd):
    w = ones_bd.shape[0]
    return jnp.concatenate(
        [_mm(x[:, i:i + w], ones_bd) for i in range(0, x.shape[-1], w)], axis=-1)


def _rk_out_kernel(of_ref, ob_ref, r_ref, k_ref, v_ref, g_ref, a_ref, x_ref, ctx_ref,
                   modl_ref, modc_ref, modl1_ref, modc1_ref, ng1_ref,
                   ka_ref, rk_ref, lng_ref, lnb_ref, w_ref,
                   x1_ref, ctx1_ref, h1_ref, *, n_ctx_tiles):
    j = pl.program_id(1)
    hw = min(MXU_DIM, r_ref.shape[-1])
    rl = lax.broadcasted_iota(jnp.int32, (hw, hw), 0)
    cl = lax.broadcasted_iota(jnp.int32, (hw, hw), 1)
    ones_bd = ((rl // RWKV_HEAD) == (cl // RWKV_HEAD)).astype(BF16)
    inv_n = 1.0 / RWKV_HEAD

    groups = [pl.ds(i, hw) for i in range(0, r_ref.shape[-1], hw)]
    f32 = lambda ref, *idx: ref[idx].astype(F32)
    o = [f32(of_ref, slice(None), gs) + f32(ob_ref, slice(None), gs) for gs in groups]
    mu = [_mm(og, ones_bd) * inv_n for og in o]
    rkk = []
    for gs in groups:
        a_sum = f32(a_ref, 0, slice(None), gs) + f32(a_ref, 1, slice(None), gs)
        ksum = f32(k_ref, slice(None), gs) * (2.0 + (a_sum - 2.0) * ka_ref[:, gs])
        rkk.append(_mm(f32(r_ref, slice(None), gs) * ksum * rk_ref[:, gs], ones_bd))
    oc, var = [], []

    def centre(gi):
        if gi < len(groups):
            oc.append(o[gi] - mu[gi])
            var.append(_mm(oc[gi] * oc[gi], ones_bd) * inv_n)

    centre(0)
    centre(1)
    out = None
    for gi, gs in enumerate(groups):
        centre(gi + 2)
        on = oc[gi] * lax.rsqrt(var[gi] + RWKV_GN_EPS) * lng_ref[:, gs] + lnb_ref[:, gs]
        y = (on + rkk[gi] * f32(v_ref, slice(None), gs)) * _silu(f32(g_ref, slice(None), gs))
        part = _mm(y, w_ref[gs, :])
        out = part if out is None else out + part
    ng1 = ng1_ref[...]

    @pl.when(j < n_ctx_tiles)
    def _():
        rows = pl.ds(pl.multiple_of(j * OUT_TILE, OUT_TILE), OUT_TILE)
        new = ctx_ref[rows, :] + modc_ref[2:3, :] * out
        ctx1_ref[rows, :] = new
        h1_ref[...] = _norm_mod(new, ng1, modc1_ref[...]).astype(BF16)

    @pl.when(j >= n_ctx_tiles)
    def _():
        new = x_ref[...] + modl_ref[2:3, :] * out
        x1_ref[...] = new
        h1_ref[...] = _norm_mod(new, ng1, modl1_ref[...]).astype(BF16)


def _rk_out(o_f, o_b, proj, a, x, ctx, mod0, mod1, norm_g1, k_a, r_k, ln_g, ln_b, w_out):
    b, l, d = x.shape
    lc = ctx.shape[1]
    lt = lc + l
    t = OUT_TILE
    nct = lc // t
    tok = pl.BlockSpec((None, t, d), lambda bi, j: (bi, j, 0))
    ptok = lambda p: pl.BlockSpec((None, None, t, d), lambda bi, j: (p, bi, j, 0))
    vec = pl.BlockSpec((1, d), lambda bi, j: (0, 0))
    lat_blk = lambda bi, j: (bi, jnp.maximum(j - nct, 0), 0)
    mod_l = pl.BlockSpec((None, 3, d), lambda bi, j: (bi, 0, 0))
    mod_c = pl.BlockSpec((None, 3, d), lambda bi, j: (b, 0, 0))
    return pl.pallas_call(
        functools.partial(_rk_out_kernel, n_ctx_tiles=nct),
        out_shape=(jax.ShapeDtypeStruct((b, l, d), F32),
                   jax.ShapeDtypeStruct((b, lc, d), F32),
                   jax.ShapeDtypeStruct((b, lt, d), BF16)),
        grid=(b, lt // t),
        in_specs=[tok, tok, ptok(0), ptok(1), ptok(2), ptok(3),
                  pl.BlockSpec((2, None, t, d), lambda bi, j: (0, bi, j, 0)),
                  pl.BlockSpec((None, t, d), lat_blk),
                  pl.BlockSpec((None, lc, d), lambda bi, j: (bi, 0, 0)),
                  mod_l, mod_c, mod_l, mod_c, vec, vec, vec, vec, vec,
                  pl.BlockSpec((d, d), lambda bi, j: (0, 0))],
        out_specs=(pl.BlockSpec((None, t, d), lat_blk),
                   pl.BlockSpec((None, lc, d), lambda bi, j: (bi, 0, 0)),
                   pl.BlockSpec((None, t, d), lambda bi, j: (bi, j, 0))),
        compiler_params=_cparams(("parallel", "arbitrary")),
        name="rk_out",
    )(o_f, o_b, proj, proj, proj, proj, a, x, ctx, mod0, mod0, mod1, mod1,
      norm_g1.reshape(1, d), k_a, r_k, ln_g, ln_b, w_out)


def _rt_proj_kernel(h_ref, w_ref, cos_ref, sin_ref, o_ref, *, dk):
    n = pl.program_id(0)
    tn = w_ref.shape[-1]
    half = dk // 2
    sub = min(ROW_TILE, h_ref.shape[0])
    chunks = [pl.ds(i, sub) for i in range(0, h_ref.shape[0], sub)]

    def project(rows):
        return jnp.dot(h_ref[rows, :], w_ref[...], preferred_element_type=F32)

    def rotate(scale):
        for rows in chunks:
            acc = project(rows)
            cos = cos_ref[rows, :]
            sin = sin_ref[rows, :]
            parts = []
            for h in range(tn // dk):
                x1 = acc[:, h * dk:h * dk + half]
                x2 = acc[:, h * dk + half:(h + 1) * dk]
                parts.append((x1 * cos - x2 * sin) * scale)
                parts.append((x1 * sin + x2 * cos) * scale)
            o_ref[rows, :] = jnp.concatenate(parts, axis=-1).astype(BF16)

    @pl.when(n == 0)
    def _():
        rotate(1.0)

    @pl.when(n == 1)
    def _():
        rotate(dk ** -0.5)

    @pl.when(n > 1)
    def _():
        for rows in chunks:
            o_ref[rows, :] = project(rows).astype(BF16)


def _rt_proj(h1, w_in, cos, sin, dk):
    m, d = h1.shape
    n_out = w_in.shape[1]
    tn = d
    tm = RT_PROJ_TILE if m % RT_PROJ_TILE == 0 else MM_TILE
    half = dk // 2
    return pl.pallas_call(
        functools.partial(_rt_proj_kernel, dk=dk),
        out_shape=jax.ShapeDtypeStruct((m, n_out), BF16),
        grid=(n_out // tn, m // tm),
        in_specs=[
            pl.BlockSpec((tm, d), lambda n, i: (i, 0)),
            pl.BlockSpec((d, tn), lambda n, i: (0, n)),
            pl.BlockSpec((tm, half), lambda n, i: (i, 0)),
            pl.BlockSpec((tm, half), lambda n, i: (i, 0)),
        ],
        out_specs=pl.BlockSpec((tm, tn), lambda n, i: (i, n)),
        compiler_params=_cparams(("parallel", "parallel")),
        name="rt_proj",
    )(h1, w_in, cos, sin)


def _log_sigmoid(x):
    return jnp.minimum(x, 0.0) - jnp.log1p(jnp.exp(-jnp.abs(x)))


def _ret_bstate_kernel(lg_ref, k_ref, v_ref, rb_ref, r_ref, *, dk, dv):
    c = k_ref.shape[0]
    s = pl.program_id(1)

    @pl.when(s == 0)
    def _():
        r_ref[...] = jnp.zeros_like(r_ref)

    pos = lax.broadcasted_iota(jnp.int32, (c, 1), 0).astype(F32)
    for h in range(RET_HEADS):
        lgb = _log_sigmoid(lg_ref[1, h])[:, 0:1]
        k = k_ref[:, h * dk:(h + 1) * dk]
        v = v_ref[:, h * dv:(h + 1) * dv]
        r0 = r_ref[h]
        rb_ref[h] = r0.astype(BF16)
        kd = (k.astype(F32) * jnp.exp(pos * lgb)).astype(BF16)
        r_ref[h] = r0 * jnp.exp(float(c) * lgb) + _dot(kd, v, _TN)


def _ret_main_kernel(lg_ref, q_ref, k_ref, v_ref, rb_ref, gng_ref, y_ref, rf_ref, dm_ref, dvec_ref,
                     *, dk, dv, n_ctx_blocks):
    c = q_ref.shape[0]
    s = pl.program_id(1)

    @pl.when(s == 0)
    def _():
        rf_ref[...] = jnp.zeros_like(rf_ref)
        diff = (lax.broadcasted_iota(jnp.int32, (c, c), 0)
                - lax.broadcasted_iota(jnp.int32, (c, c), 1)).astype(F32)
        pos = lax.broadcasted_iota(jnp.int32, (c, LANE), 0).astype(F32)
        for h in range(RET_HEADS):
            lgf = _log_sigmoid(lg_ref[0, h])
            lgb = _log_sigmoid(lg_ref[1, h])
            dm_ref[h] = jnp.where(diff >= 0, jnp.exp(jnp.maximum(diff, 0.0) * lgf[:, 0:1]),
                                  jnp.exp(jnp.maximum(-diff, 0.0) * lgb[:, 0:1]))
            dvec_ref[h, 0] = jnp.exp((pos + 1.0) * lgf)
            dvec_ref[h, 1] = jnp.exp((float(c) - pos) * lgb)
            dvec_ref[h, 2] = jnp.exp((float(c) - 1.0 - pos) * lgf)

    def lanes(x, width):
        return x[:, :width] if width <= LANE else jnp.concatenate([x] * (width // LANE), axis=1)

    @pl.when(s >= n_ctx_blocks)
    def _():
        scores = []
        for h in range(RET_HEADS):
            q = q_ref[:, h * dk:(h + 1) * dk]
            k = k_ref[:, h * dk:(h + 1) * dk]
            scores.append(_dot(q, k, _NT))
        outs = []
        for h in range(RET_HEADS):
            q = q_ref[:, h * dk:(h + 1) * dk]
            sc = (scores[h] * dm_ref[h]).astype(BF16)
            lhs = jnp.concatenate([sc, q * lanes(dvec_ref[h, 0], dk).astype(BF16),
                                   q * lanes(dvec_ref[h, 1], dk).astype(BF16)], axis=1)
            rhs = jnp.concatenate([v_ref[:, h * dv:(h + 1) * dv], rf_ref[h].astype(BF16), rb_ref[h]],
                                  axis=0)
            outs.append(_dot(lhs, rhs, _NN))
        for h in range(RET_HEADS):
            o = outs[h]
            on = o * lax.rsqrt(jnp.mean(o * o, axis=-1, keepdims=True) + EPS)
            y_ref[:, h * dv:(h + 1) * dv] = (on * gng_ref[:, h * dv:(h + 1) * dv]).astype(BF16)

    for h in range(RET_HEADS):
        k = k_ref[:, h * dk:(h + 1) * dk]
        v = v_ref[:, h * dv:(h + 1) * dv]
        kd = (k.astype(F32) * lanes(dvec_ref[h, 2], dk)).astype(BF16)
        gamma_c = jnp.exp(float(c) * _log_sigmoid(lg_ref[0, h]))
        rf_ref[h] = rf_ref[h] * lanes(gamma_c, dv) + _dot(kd, v, _TN)


def _ret(qkvg, decay_logit, gn_g, n_ctx_blocks, dk, dv):
    b, lt, _ = qkvg.shape
    c = RET_CHUNK
    nblk = lt // c
    nh = RET_HEADS
    d = nh * dk
    vd = nh * dv
    assert (2 * d) % vd == 0
    voff = (2 * d) // vd
    lg = jnp.broadcast_to(decay_logit.astype(F32)[:, :, None, None], (2, nh, 1, LANE))
    lg_spec = pl.BlockSpec((2, nh, 1, LANE), lambda bi, s: (0, 0, 0, 0))

    def bwd_blk(s):
        return jnp.where(s < n_ctx_blocks, n_ctx_blocks - 1 - s, nblk - 1 - (s - n_ctx_blocks))

    r_b = pl.pallas_call(
        functools.partial(_ret_bstate_kernel, dk=dk, dv=dv),
        out_shape=jax.ShapeDtypeStruct((b, nblk, nh, dk, dv), BF16),
        grid=(b, nblk),
        in_specs=[
            lg_spec,
            pl.BlockSpec((None, c, d), lambda bi, s: (bi, bwd_blk(s), 1)),
            pl.BlockSpec((None, c, vd), lambda bi, s: (bi, bwd_blk(s), voff)),
        ],
        out_specs=pl.BlockSpec((None, None, nh, dk, dv), lambda bi, s: (bi, bwd_blk(s), 0, 0, 0)),
        scratch_shapes=[pltpu.VMEM((nh, dk, dv), F32)],
        compiler_params=_cparams(("parallel", "arbitrary")),
        name="ret_bstate",
    )(lg, qkvg, qkvg)

    lat = lambda bi, s: (bi, jnp.maximum(s - n_ctx_blocks, 0), 0)
    return pl.pallas_call(
        functools.partial(_ret_main_kernel, dk=dk, dv=dv, n_ctx_blocks=n_ctx_blocks),
        out_shape=jax.ShapeDtypeStruct((b, lt - n_ctx_blocks * c, vd), BF16),
        grid=(b, nblk),
        in_specs=[
            lg_spec,
            pl.BlockSpec((None, c, d), lambda bi, s: (bi, s, 0)),
            pl.BlockSpec((None, c, d), lambda bi, s: (bi, s, 1)),
            pl.BlockSpec((None, c, vd), lambda bi, s: (bi, s, voff)),
            pl.BlockSpec((None, None, nh, dk, dv), lambda bi, s: (bi, s, 0, 0, 0)),
            pl.BlockSpec((1, vd), lambda bi, s: (0, 0)),
        ],
        out_specs=pl.BlockSpec((None, c, vd), lat),
        scratch_shapes=[pltpu.VMEM((nh, dk, dv), F32), pltpu.VMEM((nh, c, c), F32),
                        pltpu.VMEM((nh, 3, c, LANE), F32)],
        compiler_params=_cparams(("parallel", "arbitrary")),
        name="ret_main",
    )(lg, qkvg, qkvg, qkvg, r_b, gn_g.reshape(1, vd))


def _rt_out_kernel(y_ref, g_ref, x_ref, mod_ref, fg_ref, w_ref, o_ref):
    gated = y_ref[...].astype(F32) * _silu(g_ref[...].astype(F32))
    new = x_ref[...] + mod_ref[2:3, :] * _mm(gated, w_ref[...])
    o_ref[...] = new * lax.rsqrt(jnp.mean(new * new, axis=-1, keepdims=True) + EPS) * fg_ref[...]


def _rt_out(y, qkvg, x1, mod1, final_g, w_out, n_ctx_tiles):
    b, l, d = x1.shape
    vd = w_out.shape[0]
    t = ROW_TILE
    goff = (qkvg.shape[-1] - vd) // vd
    return pl.pallas_call(
        _rt_out_kernel,
        out_shape=jax.ShapeDtypeStruct((b, l, d), F32),
        grid=(b, l // t),
        in_specs=[
            pl.BlockSpec((None, t, vd), lambda bi, j: (bi, j, 0)),
            pl.BlockSpec((None, t, vd), lambda bi, j: (bi, j + n_ctx_tiles, goff)),
            pl.BlockSpec((None, t, d), lambda bi, j: (bi, j, 0)),
            pl.BlockSpec((None, 3, d), lambda bi, j: (bi, 0, 0)),
            pl.BlockSpec((1, d), lambda bi, j: (0, 0)),
            pl.BlockSpec((vd, d), lambda bi, j: (0, 0)),
        ],
        out_specs=pl.BlockSpec((None, t, d), lambda bi, j: (bi, j, 0)),
        compiler_params=_cparams(("parallel", "parallel")),
        name="rt_out",
    )(y, qkvg, x1, mod1, final_g.reshape(1, d), w_out)


def _rope_tables(l, lc, dk):
    t = jnp.arange(l)
    row = (t // GRID_W).astype(F32)
    col = (t % GRID_W).astype(F32)
    nf = dk // 4
    inv = ROPE_BASE ** (-jnp.arange(nf, dtype=F32) / nf)
    ang = jnp.concatenate([row[:, None] * inv, col[:, None] * inv], axis=-1)
    cos = jnp.concatenate([jnp.ones((lc, dk // 2), F32), jnp.cos(ang)], axis=0)
    sin = jnp.concatenate([jnp.zeros((lc, dk // 2), F32), jnp.sin(ang)], axis=0)
    return cos, sin


def kernel(x, c, ctx, c_ctx, ada_w, ada_b, norm_g, rk_mix, rk_w_in, rk_w0, rk_w1, rk_w2, rk_a0, rk_a1,
           rk_a2, rk_k_k, rk_k_a, rk_r_k, rk_ln_g, rk_ln_b, rk_w_out, rt_w_in, rt_decay_logit, rt_gn_g,
           rt_w_out, final_g):
    b, l, d = x.shape
    lc = ctx.shape[1]
    lt = lc + l
    assert lc == ROW_TILE and l % ROW_TILE == 0 and ROW_TILE % GRID_W == 0
    assert d % min(WKV_LANES, d) == 0 and d % WKV_GROUP == 0 and (b * lt) % MM_TILE == 0
    assert ada_w.shape[0] == 2 and rk_w_in.shape[0] == 1 and rt_w_in.shape[0] == 1
    dk = d // RET_HEADS
    dv = rt_w_out.shape[1] // RET_HEADS

    n_mod = -(-(b + 1) // 8) * 8
    cv = jnp.concatenate([c, c_ctx[None], jnp.zeros((n_mod - b - 1, d), F32)], axis=0)
    mod = _adaln(cv, ada_w, ada_b).reshape(2, n_mod, 3, d)

    xm, lw, a = _rk_pro(x, ctx, mod[0], norm_g[0], rk_mix[0], rk_w1[0].astype(BF16),
                        rk_w2[0].astype(BF16), rk_a1[0].astype(BF16), rk_a2[0].astype(BF16),
                        rk_w0[0], rk_a0[0])
    proj = _rk_proj(xm.reshape(4, b * lt, d), rk_w_in[0].astype(BF16)).reshape(4, b, lt, d)
    k_k = rk_k_k[0].reshape(1, d)
    k_a = rk_k_a[0].reshape(1, d)
    n_ctx_blocks = lc // ROW_TILE
    o_f = _wkv(proj, lw, a, k_k, k_a, 0, n_ctx_blocks)
    o_b = _wkv(proj, lw, a, k_k, k_a, 1, n_ctx_blocks)
    x1, ctx1, h1 = _rk_out(o_f, o_b, proj, a, x, ctx, mod[0], mod[1], norm_g[1], k_a,
                           rk_r_k[0].reshape(1, d), rk_ln_g[0].reshape(1, d), rk_ln_b[0].reshape(1, d),
                           rk_w_out[0].astype(BF16))
    del ctx1

    cos, sin = _rope_tables(l, lc, dk)
    qkvg = _rt_proj(h1.reshape(b * lt, d), rt_w_in[0].astype(BF16), jnp.tile(cos, (b, 1)),
                    jnp.tile(sin, (b, 1)), dk)
    y = _ret(qkvg.reshape(b, lt, -1), rt_decay_logit[0], rt_gn_g[0], lc // RET_CHUNK, dk, dv)
    return _rt_out(y, qkvg.reshape(b, lt, -1), x1, mod[1], final_g, rt_w_out[0].astype(BF16),
                   lc // ROW_TILE)
```

```python
import functools
import math

import jax
import jax.numpy as jnp
from jax import lax
from jax.experimental import pallas as pl
from jax.experimental.pallas import tpu as pltpu

F32 = jnp.float32
BF16 = jnp.bfloat16

GRID_W = 64
EPS = 1e-6
RWKV_HEAD = 64
RWKV_GN_EPS = 64e-5
RET_HEADS = 8
ROPE_BASE = 10000.0

LANE = 128
MXU_DIM = 256
VMEM_LIMIT = 56 * 1024 * 1024

ROW_TILE = 256
OUT_TILE = 128
RK_OUT_SPLITS = 2
MM_TILE = 512
RT_PROJ_TILE = 1024
WKV_CHUNK = 64
WKV_GROUP = MXU_DIM
WKV_STAGGER = 3
WKV_LANES = 2048
RET_CHUNK = 256
ACT_DTYPE = BF16


def _cparams(sem, vmem=VMEM_LIMIT):
    return pltpu.CompilerParams(dimension_semantics=sem, vmem_limit_bytes=vmem)


def _mm(a, b):
    return jnp.dot(a.astype(BF16), b.astype(BF16), preferred_element_type=F32)


def _mm_nt(a, b):
    return lax.dot_general(a.astype(BF16), b.astype(BF16), (((1,), (1,)), ((), ())),
                           preferred_element_type=F32)


def _mm_tn(a, b):
    return lax.dot_general(a.astype(BF16), b.astype(BF16), (((0,), (0,)), ((), ())),
                           preferred_element_type=F32)


def _split2(x):
    hi = x.astype(BF16)
    lo = (x - hi.astype(F32)).astype(BF16)
    return hi, lo


def _split3(x):
    hi = x.astype(BF16)
    r = x - hi.astype(F32)
    mid = r.astype(BF16)
    lo = (r - mid.astype(F32)).astype(BF16)
    return hi, mid, lo


def _dot(a, b, dims):
    return lax.dot_general(a, b, dims, preferred_element_type=F32)


_NN = (((1,), (0,)), ((), ()))
_NT = (((1,), (1,)), ((), ()))
_TN = (((0,), (0,)), ((), ()))


def _mm3(a, b, dims=_NN):
    a1, a2 = _split2(a)
    b1, b2 = _split2(b)
    return _dot(a1, b1, dims) + (_dot(a1, b2, dims) + _dot(a2, b1, dims))


def _mm_exact_lhs(a_bf16, b):
    b1, b2, b3 = _split3(b)
    return _dot(a_bf16, b1, _NN) + (_dot(a_bf16, b2, _NN) + _dot(a_bf16, b3, _NN))


def _mm_exact_rhs(a, b_bf16):
    a1, a2, a3 = _split3(a)
    return _dot(a1, b_bf16, _NN) + (_dot(a2, b_bf16, _NN) + _dot(a3, b_bf16, _NN))


def _sigmoid(x):
    return 0.5 * jnp.tanh(0.5 * x) + 0.5


def _silu(x):
    h = 0.5 * x
    return h + h * jnp.tanh(h)


def _norm_mod(xb, g, mod):
    y = xb * lax.rsqrt(jnp.mean(xb * xb, axis=-1, keepdims=True) + EPS) * g
    return y * (1.0 + mod[1:2]) + mod[0:1]


def _adaln_kernel(cv_ref, w_ref, b_ref, o_ref):
    s = _silu(cv_ref[...])
    o_ref[...] = _mm3(s, w_ref[...]) + b_ref[...]


def _adaln(cv, ada_w, ada_b):
    depth, d, d3 = ada_w.shape
    rows = cv.shape[0]
    tn = math.gcd(d3, 6 * LANE)
    return pl.pallas_call(
        _adaln_kernel,
        out_shape=jax.ShapeDtypeStruct((depth, rows, d3), F32),
        grid=(depth, d3 // tn),
        in_specs=[
            pl.BlockSpec((rows, d), lambda i, n: (0, 0)),
            pl.BlockSpec((None, d, tn), lambda i, n: (i, 0, n)),
            pl.BlockSpec((None, 1, tn), lambda i, n: (i, 0, n)),
        ],
        out_specs=pl.BlockSpec((None, rows, tn), lambda i, n: (i, 0, n)),
        compiler_params=_cparams(("parallel", "parallel")),
        name="adaln",
    )(cv, ada_w, ada_b.reshape(depth, 1, d3))


def _rk_pro_kernel(ctx_ref, x_ref, xp_ref, xn_ref, modc_ref, modl_ref, g_ref, mix_ref,
                   w1_ref, w2_ref, a1_ref, a2_ref, w0_ref, a0_ref, o_ref, lw_ref, a_ref,
                   *, n_lat_tiles):
    j = pl.program_id(1)
    g = g_ref[...]
    mix = mix_ref[...]
    t, d = x_ref.shape
    row = lax.broadcasted_iota(jnp.int32, (t, 1), 0)
    n_proj = o_ref.shape[0]

    def emit(h, hs):
        delta = hs - h
        for p in range(n_proj):
            o_ref[p] = (h + delta * mix[p:p + 1]).astype(BF16)
        xw = (h + delta * mix[n_proj:n_proj + 1]).astype(BF16)
        xa = (h + delta * mix[n_proj + 1:n_proj + 2]).astype(BF16)
        for n in range(2):
            hw = jnp.tanh(jnp.dot(xw, w1_ref[n], preferred_element_type=F32))
            ww = w0_ref[n:n + 1, :] + _mm(hw, w2_ref[n])
            half_c = -0.5 * math.exp(-0.5)
            lw_ref[n] = half_c * jnp.tanh(0.5 * ww) + half_c
            ha = jnp.dot(xa, a1_ref[n], preferred_element_type=F32)
            a_ref[n] = _sigmoid(a0_ref[n:n + 1, :] + _mm(ha, a2_ref[n])).astype(a_ref.dtype)

    @pl.when(j == 0)
    def _():
        h = _norm_mod(ctx_ref[...], g, modc_ref[...])
        half = d // 2
        prev = jnp.where(row == 0, 0.0, pltpu.roll(h[:, :half], 1, 0))
        nxt = jnp.where(row == t - 1, 0.0, pltpu.roll(h[:, half:], t - 1, 0))
        emit(h, jnp.concatenate([prev, nxt], axis=-1))

    @pl.when(j > 0)
    def _():
        mod = modl_ref[...]
        h = _norm_mod(x_ref[...], g, mod)
        hp = _norm_mod(xp_ref[...], g, mod)
        hn = _norm_mod(xn_ref[...], g, mod)
        q = d // 4
        col = row % GRID_W
        left = jnp.where(col == 0, 0.0, pltpu.roll(h[:, :q], 1, 0))
        right = jnp.where(col == GRID_W - 1, 0.0, pltpu.roll(h[:, q:2 * q], t - 1, 0))
        up = jnp.concatenate([hp[:, 2 * q:3 * q], h[:t - GRID_W, 2 * q:3 * q]], axis=0)
        up = jnp.where((j == 1) & (row < GRID_W), 0.0, up)
        down = jnp.concatenate([h[GRID_W:, 3 * q:], hn[:, 3 * q:]], axis=0)
        down = jnp.where((j == n_lat_tiles) & (row >= t - GRID_W), 0.0, down)
        emit(h, jnp.concatenate([left, right, up, down], axis=-1))


def _rk_pro(x, ctx, mod0, norm_g0, mix, w1, w2, a1, a2, w0, a0):
    b, l, d = x.shape
    t = ROW_TILE
    n_lat = l // t
    sub = t // GRID_W
    n_rows = l // GRID_W
    xr = x.reshape(b, n_rows, GRID_W, d)
    nmix = mix.shape[0]
    n_proj = nmix - 2
    r = w1.shape[-1]
    lt = ctx.shape[1] + l
    full = lambda shape: pl.BlockSpec(shape, lambda bi, j: (0,) * len(shape))
    tok2 = pl.BlockSpec((2, None, t, d), lambda bi, j: (0, bi, j, 0))
    return pl.pallas_call(
        functools.partial(_rk_pro_kernel, n_lat_tiles=n_lat),
        out_shape=(jax.ShapeDtypeStruct((n_proj, b, lt, d), BF16),
                   jax.ShapeDtypeStruct((2, b, lt, d), F32),
                   jax.ShapeDtypeStruct((2, b, lt, d), ACT_DTYPE)),
        grid=(b, n_lat + 1),
        in_specs=[
            pl.BlockSpec((None, t, d), lambda bi, j: (bi, 0, 0)),
            pl.BlockSpec((None, t, d), lambda bi, j: (bi, jnp.maximum(j - 1, 0), 0)),
            pl.BlockSpec((None, None, GRID_W, d),
                         lambda bi, j: (bi, jnp.maximum((j - 1) * sub - 1, 0), 0, 0)),
            pl.BlockSpec((None, None, GRID_W, d),
                         lambda bi, j: (bi, jnp.minimum(jnp.maximum(j, 1) * sub, n_rows - 1), 0, 0)),
            pl.BlockSpec((None, 3, d), lambda bi, j: (b, 0, 0)),
            pl.BlockSpec((None, 3, d), lambda bi, j: (bi, 0, 0)),
            pl.BlockSpec((1, d), lambda bi, j: (0, 0)),
            pl.BlockSpec((nmix, d), lambda bi, j: (0, 0)),
            full((2, d, r)), full((2, r, d)), full((2, d, r)), full((2, r, d)),
            full((2, d)), full((2, d)),
        ],
        out_specs=(pl.BlockSpec((n_proj, None, t, d), lambda bi, j: (0, bi, j, 0)), tok2, tok2),
        compiler_params=_cparams(("parallel", "parallel")),
        name="rk_pro",
    )(ctx, x, xr, xr, mod0, mod0, norm_g0.reshape(1, d), mix, w1, w2, a1, a2, w0, a0)


def _rk_proj_kernel(x_ref, w_ref, o_ref):
    o_ref[...] = jnp.dot(x_ref[...], w_ref[...], preferred_element_type=F32).astype(o_ref.dtype)


def _rk_proj(xm, w_in):
    nproj, d, _ = w_in.shape
    m = xm.shape[1]
    tm = RT_PROJ_TILE if m % RT_PROJ_TILE == 0 else MM_TILE
    return pl.pallas_call(
        _rk_proj_kernel,
        out_shape=jax.ShapeDtypeStruct((nproj, m, d), ACT_DTYPE),
        grid=(nproj, m // tm),
        in_specs=[
            pl.BlockSpec((None, tm, d), lambda p, i: (p, i, 0)),
            pl.BlockSpec((None, d, d), lambda p, i: (p, 0, 0)),
        ],
        out_specs=pl.BlockSpec((None, tm, d), lambda p, i: (p, i, 0)),
        compiler_params=_cparams(("parallel", "parallel")),
        name="rk_proj",
    )(xm, w_in)


def _wkv_kernel(r_ref, k_ref, v_ref, lw_ref, a_ref, kk_ref, ka_ref, *rest, reverse, chunks_per_block):
    prev_ref = rest[0] if len(rest) == 3 else None
    o_ref, t_ref = rest[-2:]
    c = WKV_CHUNK
    gw = WKV_GROUP
    nq = r_ref.shape[-1] // gw
    nh = gw // RWKV_HEAD
    hb = gw
    s = pl.program_id(2)

    @pl.when(s == 0)
    def _():
        t_ref[...] = jnp.zeros_like(t_ref)

    row_id = lax.broadcasted_iota(jnp.int32, (c, 1), 0)

    def running_sum(x):
        sh = 1
        while sh < c:
            if reverse:
                x = x + jnp.where(row_id < c - sh, pltpu.roll(x, c - sh, 0), 0.0)
            else:
                x = x + jnp.where(row_id >= sh, pltpu.roll(x, sh, 0), 0.0)
            sh *= 2
        return x

    rw = lax.broadcasted_iota(jnp.int32, (c, nh * c), 0)
    cw = lax.broadcasted_iota(jnp.int32, (c, nh * c), 1) % c
    strict_w = (cw > rw) if reverse else (cw < rw)
    incl_w = strict_w | (cw == rw)
    eye_w = (cw == rw).astype(F32)
    rs = lax.broadcasted_iota(jnp.int32, (nh * c, nh * c), 0)
    cs = lax.broadcasted_iota(jnp.int32, (nh * c, nh * c), 1)
    bd_mask = ((rs // c) == (cs // c)).astype(BF16)
    rl = lax.broadcasted_iota(jnp.int32, (hb, hb), 0)
    cl = lax.broadcasted_iota(jnp.int32, (hb, hb), 1)
    head_bd = (rl // RWKV_HEAD) == (cl // RWKV_HEAD)
    ones_bd = head_bd.astype(BF16)
    lane_head = lax.broadcasted_iota(jnp.int32, (1, hb), 1) // RWKV_HEAD
    lane_masks = [(lane_head == h).astype(BF16) for h in range(nh)]
    end_row = 0 if reverse else c - 1

    def stack(x):
        xb = x.astype(BF16)
        return jnp.concatenate([xb * m for m in lane_masks], axis=0)

    def block_diag(xw):
        return jnp.concatenate([xw.astype(BF16)] * nh, axis=0) * bd_mask

    def prepass(cidx, q, out):
        rows = pl.ds(cidx * c, c)
        lanes = pl.ds(q * gw, gw)
        r = r_ref[rows, lanes].astype(F32)
        k = k_ref[rows, lanes].astype(F32)
        v = v_ref[rows, lanes].astype(F32)
        lw = lw_ref[rows, lanes]
        a = a_ref[rows, lanes].astype(F32)
        kap = ka_ref[:, lanes]

        b = running_sum(lw)
        kk = k * kk_ref[:, lanes]
        ssq = ssq_blocks[q][cidx * c:(cidx + 1) * c]
        b_end = b[end_row:end_row + 1, :]
        e_pos = jnp.exp(b)
        e_neg = jnp.exp(-b)
        e_prev = jnp.exp(b - lw)
        e_end = jnp.exp(b_end - b)
        p_end = jnp.exp(b_end)

        kappa = kk / jnp.maximum(jnp.sqrt(ssq), 1e-12)
        kd = k * (1.0 + (a - 1.0) * kap)
        beta = kappa * a

        kt = kappa * e_prev
        rt = r * e_pos
        bt_s = stack(beta * e_neg)
        kn_s = stack(kd * e_neg)
        v_s = stack(v)
        bbar = beta * e_end
        kbar = kd * e_end

        lhs = jnp.concatenate([kt, rt], axis=0)
        ab = _mm_nt(lhs, bt_s)
        ak = _mm_nt(lhs, kn_s)
        yield
        a_ab = jnp.where(strict_w, ab[:c], 0.0)
        a_rb = jnp.where(incl_w, ab[c:], 0.0)
        a_ak = jnp.where(strict_w, ak[:c], 0.0)
        a_rk = jnp.where(incl_w, ak[c:], 0.0)

        pw = -a_ab
        minv = eye_w + pw
        pw = _mm(pw, block_diag(pw))
        intra_v = _mm(jnp.concatenate([a_ak, a_rk], axis=0), v_s)
        av = intra_v[:c]
        ark_v = intra_v[c:]
        yield
        n_sq = int(math.log2(c)) - 1
        for i in range(n_sq):
            both = _mm(jnp.concatenate([minv, pw], axis=0) if i < n_sq - 1 else minv, block_diag(pw))
            minv = minv + both[:c]
            pw = both[c:]
            yield

        keys = jnp.concatenate([kbar, bbar], axis=0).astype(BF16)
        out[(cidx, q)] = (lhs.astype(BF16), minv.astype(BF16), a_rb.astype(BF16), av, ark_v,
                          v.astype(BF16), keys, p_end)

    order = list(range(chunks_per_block))
    if reverse:
        order.reverse()
    factors = {}
    ssq_blocks = []
    for q in range(nq):
        kk_blk = k_ref[:, pl.ds(q * gw, gw)].astype(F32) * kk_ref[:, pl.ds(q * gw, gw)]
        ssq_blocks.append(_mm(kk_blk * kk_blk, ones_bd))
    def recurrence():
        for cidx in order:
            while any((cidx, q) not in factors for q in range(nq)):
                yield
            rows = pl.ds(cidx * c, c)
            fac = [factors[(cidx, q)] for q in range(nq)]
            reads = [_dot(f[0], t_ref[q].astype(BF16), _NT) for q, f in enumerate(fac)]
            yield
            us = [-_dot(f[1], stack(reads[q][:c] + f[3]), _NN) for q, f in enumerate(fac)]
            yield
            for q, f in enumerate(fac):
                _, _, a_rb, _, ark_v, vb, keys, p_end = f
                o = reads[q][c:] + ark_v + _dot(a_rb, stack(us[q]), _NN)
                if prev_ref is not None:
                    o = o + prev_ref[rows, pl.ds(q * gw, gw)].astype(F32)
                o_ref[rows, pl.ds(q * gw, gw)] = o.astype(o_ref.dtype)
                vu = jnp.concatenate([vb, us[q].astype(BF16)], axis=0)
                t_ref[q] = t_ref[q] * p_end + jnp.where(head_bd, _dot(vu, keys, _TN), 0.0)
            yield

    chains = [prepass(cidx, q, factors) for cidx in order for q in range(nq)]
    start = [(i // nq) * WKV_STAGGER for i in range(len(chains))]
    live = list(range(len(chains)))
    scan = recurrence()
    rnd = 0
    while scan is not None:
        live = [i for i in live if start[i] > rnd or next(chains[i], "done") != "done"]
        if next(scan, "done") == "done":
            scan = None
        rnd += 1


def _wkv(proj, lw, a, k_k, k_a, direction, n_ctx_blocks, prev=None):
    _, b, lt, d = proj.shape
    rb = ROW_TILE
    hb = min(WKV_LANES, d)
    nblk = lt // rb
    reverse = direction == 1

    if reverse:
        def blk(s):
            return jnp.where(s < n_ctx_blocks, n_ctx_blocks - 1 - s, nblk - 1 - (s - n_ctx_blocks))
    else:
        def blk(s):
            return s

    def tok(p):
        return pl.BlockSpec((None, None, rb, hb), lambda bi, hi, s: (p, bi, blk(s), hi))

    out_spec = pl.BlockSpec((None, rb, hb), lambda bi, hi, s: (bi, blk(s), hi))
    extra = () if prev is None else (prev,)
    return pl.pallas_call(
        functools.partial(_wkv_kernel, reverse=reverse, chunks_per_block=rb // WKV_CHUNK),
        out_shape=jax.ShapeDtypeStruct((b, lt, d), ACT_DTYPE),
        grid=(b, d // hb, nblk),
        in_specs=[tok(0), tok(1), tok(2), tok(direction), tok(direction),
                  pl.BlockSpec((1, hb), lambda bi, hi, s: (0, hi)),
                  pl.BlockSpec((1, hb), lambda bi, hi, s: (0, hi))] + [out_spec] * len(extra),
        out_specs=out_spec,
        scratch_shapes=[pltpu.VMEM((hb // WKV_GROUP, WKV_GROUP, WKV_GROUP), F32)],
        compiler_params=_cparams(("parallel", "parallel", "arbitrary")),
        name="wkv_bwd" if reverse else "wkv_fwd",
    )(proj, proj, proj, lw, a, k_k, k_a, *extra)


def _head_sums(x, ones_bd):
    w = ones_bd.shape[0]
    return jnp.concatenate(
        [_mm(x[:, i:i + w], ones_bd) for i in range(0, x.shape[-1], w)], axis=-1)


def _rk_out_kernel(o_ref, r_ref, k_ref, v_ref, g_ref, a_ref, x_ref, ctx_ref,
                   modl_ref, modc_ref, modl1_ref, modc1_ref, ng1_ref,
                   ka_ref, rk_ref, lng_ref, lnb_ref, w_ref,
                   x1_ref, ctx1_ref, h1_ref, *, n_ctx_tiles):
    j = pl.program_id(1)
    hw = min(MXU_DIM, r_ref.shape[-1])
    rl = lax.broadcasted_iota(jnp.int32, (hw, hw), 0)
    cl = lax.broadcasted_iota(jnp.int32, (hw, hw), 1)
    same_head = (rl // RWKV_HEAD) == (cl // RWKV_HEAD)
    ones_bd = same_head.astype(BF16)
    mean_bd = jnp.where(same_head, 1.0 / RWKV_HEAD, 0.0).astype(BF16)

    groups = [pl.ds(i, hw) for i in range(0, r_ref.shape[-1], hw)]
    f32 = lambda ref, *idx: ref[idx].astype(F32)
    o = [f32(o_ref, slice(None), gs) for gs in groups]
    mu = [_mm(og, mean_bd) for og in o]
    rkk = []
    for gs in groups:
        ka = ka_ref[:, gs]
        rk = rk_ref[:, gs]
        a_sum = f32(a_ref, 0, slice(None), gs) + f32(a_ref, 1, slice(None), gs)
        scale = (2.0 - 2.0 * ka) * rk + a_sum * (ka * rk)
        rkk.append(_mm(f32(r_ref, slice(None), gs) * f32(k_ref, slice(None), gs) * scale, ones_bd))
    oc, var = [], []

    def centre(gi):
        if gi < len(groups):
            oc.append(o[gi] - mu[gi])
            var.append(_mm(oc[gi] * oc[gi], mean_bd))

    for gi in range(len(groups)):
        centre(gi)
    out = None
    per = max(1, len(groups) // RK_OUT_SPLITS)
    for g0 in range(0, len(groups), per):
        ys = []
        for gi in range(g0, min(g0 + per, len(groups))):
            gs = groups[gi]
            on = oc[gi] * lax.rsqrt(var[gi] + RWKV_GN_EPS) * lng_ref[:, gs] + lnb_ref[:, gs]
            y = (on + rkk[gi] * f32(v_ref, slice(None), gs)) * _silu(f32(g_ref, slice(None), gs))
            ys.append(y.astype(BF16))
        part = _dot(jnp.concatenate(ys, axis=-1), w_ref[pl.ds(g0 * hw, len(ys) * hw), :], _NN)
        out = part if out is None else out + part
    ng1 = ng1_ref[...]

    @pl.when(j < n_ctx_tiles)
    def _():
        rows = pl.ds(pl.multiple_of(j * OUT_TILE, OUT_TILE), OUT_TILE)
        new = ctx_ref[rows, :] + modc_ref[2:3, :] * out
        ctx1_ref[rows, :] = new
        h1_ref[...] = _norm_mod(new, ng1, modc1_ref[...]).astype(BF16)

    @pl.when(j >= n_ctx_tiles)
    def _():
        new = x_ref[...] + modl_ref[2:3, :] * out
        x1_ref[...] = new
        h1_ref[...] = _norm_mod(new, ng1, modl1_ref[...]).astype(BF16)


def _rk_out(o_sum, proj, a, x, ctx, mod0, mod1, norm_g1, k_a, r_k, ln_g, ln_b, w_out):
    b, l, d = x.shape
    lc = ctx.shape[1]
    lt = lc + l
    t = OUT_TILE
    nct = lc // t
    tok = pl.BlockSpec((None, t, d), lambda bi, j: (bi, j, 0))
    ptok = lambda p: pl.BlockSpec((None, None, t, d), lambda bi, j: (p, bi, j, 0))
    vec = pl.BlockSpec((1, d), lambda bi, j: (0, 0))
    lat_blk = lambda bi, j: (bi, jnp.maximum(j - nct, 0), 0)
    mod_l = pl.BlockSpec((None, 3, d), lambda bi, j: (bi, 0, 0))
    mod_c = pl.BlockSpec((None, 3, d), lambda bi, j: (b, 0, 0))
    return pl.pallas_call(
        functools.partial(_rk_out_kernel, n_ctx_tiles=nct),
        out_shape=(jax.ShapeDtypeStruct((b, l, d), F32),
                   jax.ShapeDtypeStruct((b, lc, d), F32),
                   jax.ShapeDtypeStruct((b, lt, d), BF16)),
        grid=(b, lt // t),
        in_specs=[tok, ptok(0), ptok(1), ptok(2), ptok(3),
                  pl.BlockSpec((2, None, t, d), lambda bi, j: (0, bi, j, 0)),
                  pl.BlockSpec((None, t, d), lat_blk),
                  pl.BlockSpec((None, lc, d), lambda bi, j: (bi, 0, 0)),
                  mod_l, mod_c, mod_l, mod_c, vec, vec, vec, vec, vec,
                  pl.BlockSpec((d, d), lambda bi, j: (0, 0))],
        out_specs=(pl.BlockSpec((None, t, d), lat_blk),
                   pl.BlockSpec((None, lc, d), lambda bi, j: (bi, 0, 0)),
                   pl.BlockSpec((None, t, d), lambda bi, j: (bi, j, 0))),
        compiler_params=_cparams(("parallel", "arbitrary")),
        name="rk_out",
    )(o_sum, proj, proj, proj, proj, a, x, ctx, mod0, mod0, mod1, mod1,
      norm_g1.reshape(1, d), k_a, r_k, ln_g, ln_b, w_out)


def _rt_proj_kernel(h_ref, w_ref, cos_ref, sin_ref, o_ref, *, dk):
    n = pl.program_id(0)
    tn = w_ref.shape[-1]
    half = dk // 2
    sub = min(ROW_TILE, h_ref.shape[0])
    chunks = [pl.ds(i, sub) for i in range(0, h_ref.shape[0], sub)]

    def project(rows):
        return jnp.dot(h_ref[rows, :], w_ref[...], preferred_element_type=F32)

    def rotate(scale):
        for rows in chunks:
            acc = project(rows)
            cos = cos_ref[rows, :]
            sin = sin_ref[rows, :]
            parts = []
            for h in range(tn // dk):
                x1 = acc[:, h * dk:h * dk + half]
                x2 = acc[:, h * dk + half:(h + 1) * dk]
                parts.append((x1 * cos - x2 * sin) * scale)
                parts.append((x1 * sin + x2 * cos) * scale)
            o_ref[rows, :] = jnp.concatenate(parts, axis=-1).astype(BF16)

    @pl.when(n == 0)
    def _():
        rotate(1.0)

    @pl.when(n == 1)
    def _():
        rotate(dk ** -0.5)

    @pl.when(n > 1)
    def _():
        for rows in chunks:
            o_ref[rows, :] = project(rows).astype(BF16)


def _rt_proj(h1, w_in, cos, sin, dk):
    m, d = h1.shape
    n_out = w_in.shape[1]
    tn = d
    tm = RT_PROJ_TILE if m % RT_PROJ_TILE == 0 else MM_TILE
    half = dk // 2
    return pl.pallas_call(
        functools.partial(_rt_proj_kernel, dk=dk),
        out_shape=jax.ShapeDtypeStruct((m, n_out), BF16),
        grid=(n_out // tn, m // tm),
        in_specs=[
            pl.BlockSpec((tm, d), lambda n, i: (i, 0)),
            pl.BlockSpec((d, tn), lambda n, i: (0, n)),
            pl.BlockSpec((tm, half), lambda n, i: (i, 0)),
            pl.BlockSpec((tm, half), lambda n, i: (i, 0)),
        ],
        out_specs=pl.BlockSpec((tm, tn), lambda n, i: (i, n)),
        compiler_params=_cparams(("parallel", "parallel")),
        name="rt_proj",
    )(h1, w_in, cos, sin)


def _log_sigmoid(x):
    return jnp.minimum(x, 0.0) - jnp.log1p(jnp.exp(-jnp.abs(x)))


def _ret_bstate_kernel(lg_ref, k_ref, v_ref, rb_ref, r_ref, *, dk, dv):
    c = k_ref.shape[0]
    s = pl.program_id(1)

    @pl.when(s == 0)
    def _():
        r_ref[...] = jnp.zeros_like(r_ref)

    pos = lax.broadcasted_iota(jnp.int32, (c, 1), 0).astype(F32)
    for h in range(RET_HEADS):
        lgb = _log_sigmoid(lg_ref[1, h])[:, 0:1]
        k = k_ref[:, h * dk:(h + 1) * dk]
        v = v_ref[:, h * dv:(h + 1) * dv]
        r0 = r_ref[h]
        rb_ref[h] = r0.astype(BF16)
        kd = (k.astype(F32) * jnp.exp(pos * lgb)).astype(BF16)
        r_ref[h] = r0 * jnp.exp(float(c) * lgb) + _dot(kd, v, _TN)


def _ret_main_kernel(lg_ref, q_ref, k_ref, v_ref, rb_ref, gng_ref, y_ref, rf_ref, dm_ref, dvec_ref,
                     *, dk, dv, n_ctx_blocks):
    c = q_ref.shape[0]
    s = pl.program_id(1)

    @pl.when(s == 0)
    def _():
        rf_ref[...] = jnp.zeros_like(rf_ref)
        diff = (lax.broadcasted_iota(jnp.int32, (c, c), 0)
                - lax.broadcasted_iota(jnp.int32, (c, c), 1)).astype(F32)
        pos = lax.broadcasted_iota(jnp.int32, (c, LANE), 0).astype(F32)
        for h in range(RET_HEADS):
            lgf = _log_sigmoid(lg_ref[0, h])
            lgb = _log_sigmoid(lg_ref[1, h])
            dm_ref[h] = jnp.where(diff >= 0, jnp.exp(jnp.maximum(diff, 0.0) * lgf[:, 0:1]),
                                  jnp.exp(jnp.maximum(-diff, 0.0) * lgb[:, 0:1]))
            dvec_ref[h, 0] = jnp.exp((pos + 1.0) * lgf)
            dvec_ref[h, 1] = jnp.exp((float(c) - pos) * lgb)
            dvec_ref[h, 2] = jnp.exp((float(c) - 1.0 - pos) * lgf)

    def lanes(x, width):
        return x[:, :width] if width <= LANE else jnp.concatenate([x] * (width // LANE), axis=1)

    @pl.when(s >= n_ctx_blocks)
    def _():
        scores = []
        for h in range(RET_HEADS):
            q = q_ref[:, h * dk:(h + 1) * dk]
            k = k_ref[:, h * dk:(h + 1) * dk]
            scores.append(_dot(q, k, _NT))
        outs = []
        for h in range(RET_HEADS):
            q = q_ref[:, h * dk:(h + 1) * dk]
            sc = (scores[h] * dm_ref[h]).astype(BF16)
            lhs = jnp.concatenate([sc, q * lanes(dvec_ref[h, 0], dk).astype(BF16),
                                   q * lanes(dvec_ref[h, 1], dk).astype(BF16)], axis=1)
            rhs = jnp.concatenate([v_ref[:, h * dv:(h + 1) * dv], rf_ref[h].astype(BF16), rb_ref[h]],
                                  axis=0)
            outs.append(_dot(lhs, rhs, _NN))
        for h in range(RET_HEADS):
            o = outs[h]
            on = o * lax.rsqrt(jnp.mean(o * o, axis=-1, keepdims=True) + EPS)
            y_ref[:, h * dv:(h + 1) * dv] = (on * gng_ref[:, h * dv:(h + 1) * dv]).astype(BF16)

    for h in range(RET_HEADS):
        k = k_ref[:, h * dk:(h + 1) * dk]
        v = v_ref[:, h * dv:(h + 1) * dv]
        kd = (k.astype(F32) * lanes(dvec_ref[h, 2], dk)).astype(BF16)
        gamma_c = jnp.exp(float(c) * _log_sigmoid(lg_ref[0, h]))
        rf_ref[h] = rf_ref[h] * lanes(gamma_c, dv) + _dot(kd, v, _TN)


def _ret(qkvg, decay_logit, gn_g, n_ctx_blocks, dk, dv):
    b, lt, _ = qkvg.shape
    c = RET_CHUNK
    nblk = lt // c
    nh = RET_HEADS
    d = nh * dk
    vd = nh * dv
    assert (2 * d) % vd == 0
    voff = (2 * d) // vd
    lg = jnp.broadcast_to(decay_logit.astype(F32)[:, :, None, None], (2, nh, 1, LANE))
    lg_spec = pl.BlockSpec((2, nh, 1, LANE), lambda bi, s: (0, 0, 0, 0))

    def bwd_blk(s):
        return jnp.where(s < n_ctx_blocks, n_ctx_blocks - 1 - s, nblk - 1 - (s - n_ctx_blocks))

    r_b = pl.pallas_call(
        functools.partial(_ret_bstate_kernel, dk=dk, dv=dv),
        out_shape=jax.ShapeDtypeStruct((b, nblk, nh, dk, dv), BF16),
        grid=(b, nblk),
        in_specs=[
            lg_spec,
            pl.BlockSpec((None, c, d), lambda bi, s: (bi, bwd_blk(s), 1)),
            pl.BlockSpec((None, c, vd), lambda bi, s: (bi, bwd_blk(s), voff)),
        ],
        out_specs=pl.BlockSpec((None, None, nh, dk, dv), lambda bi, s: (bi, bwd_blk(s), 0, 0, 0)),
        scratch_shapes=[pltpu.VMEM((nh, dk, dv), F32)],
        compiler_params=_cparams(("parallel", "arbitrary")),
        name="ret_bstate",
    )(lg, qkvg, qkvg)

    lat = lambda bi, s: (bi, jnp.maximum(s - n_ctx_blocks, 0), 0)
    return pl.pallas_call(
        functools.partial(_ret_main_kernel, dk=dk, dv=dv, n_ctx_blocks=n_ctx_blocks),
        out_shape=jax.ShapeDtypeStruct((b, lt - n_ctx_blocks * c, vd), BF16),
        grid=(b, nblk),
        in_specs=[
            lg_spec,
            pl.BlockSpec((None, c, d), lambda bi, s: (bi, s, 0)),
            pl.BlockSpec((None, c, d), lambda bi, s: (bi, s, 1)),
            pl.BlockSpec((None, c, vd), lambda bi, s: (bi, s, voff)),
            pl.BlockSpec((None, None, nh, dk, dv), lambda bi, s: (bi, s, 0, 0, 0)),
            pl.BlockSpec((1, vd), lambda bi, s: (0, 0)),
        ],
        out_specs=pl.BlockSpec((None, c, vd), lat),
        scratch_shapes=[pltpu.VMEM((nh, dk, dv), F32), pltpu.VMEM((nh, c, c), F32),
                        pltpu.VMEM((nh, 3, c, LANE), F32)],
        compiler_params=_cparams(("parallel", "arbitrary")),
        name="ret_main",
    )(lg, qkvg, qkvg, qkvg, r_b, gn_g.reshape(1, vd))


def _rt_out_kernel(y_ref, g_ref, x_ref, mod_ref, fg_ref, w_ref, o_ref):
    gated = y_ref[...].astype(F32) * _silu(g_ref[...].astype(F32))
    new = x_ref[...] + mod_ref[2:3, :] * _mm(gated, w_ref[...])
    o_ref[...] = new * lax.rsqrt(jnp.mean(new * new, axis=-1, keepdims=True) + EPS) * fg_ref[...]


def _rt_out(y, qkvg, x1, mod1, final_g, w_out, n_ctx_tiles):
    b, l, d = x1.shape
    vd = w_out.shape[0]
    t = ROW_TILE
    goff = (qkvg.shape[-1] - vd) // vd
    return pl.pallas_call(
        _rt_out_kernel,
        out_shape=jax.ShapeDtypeStruct((b, l, d), F32),
        grid=(b, l // t),
        in_specs=[
            pl.BlockSpec((None, t, vd), lambda bi, j: (bi, j, 0)),
            pl.BlockSpec((None, t, vd), lambda bi, j: (bi, j + n_ctx_tiles, goff)),
            pl.BlockSpec((None, t, d), lambda bi, j: (bi, j, 0)),
            pl.BlockSpec((None, 3, d), lambda bi, j: (bi, 0, 0)),
            pl.BlockSpec((1, d), lambda bi, j: (0, 0)),
            pl.BlockSpec((vd, d), lambda bi, j: (0, 0)),
        ],
        out_specs=pl.BlockSpec((None, t, d), lambda bi, j: (bi, j, 0)),
        compiler_params=_cparams(("parallel", "parallel")),
        name="rt_out",
    )(y, qkvg, x1, mod1, final_g.reshape(1, d), w_out)


def _rope_tables(l, lc, dk):
    t = jnp.arange(l)
    row = (t // GRID_W).astype(F32)
    col = (t % GRID_W).astype(F32)
    nf = dk // 4
    inv = ROPE_BASE ** (-jnp.arange(nf, dtype=F32) / nf)
    ang = jnp.concatenate([row[:, None] * inv, col[:, None] * inv], axis=-1)
    cos = jnp.concatenate([jnp.ones((lc, dk // 2), F32), jnp.cos(ang)], axis=0)
    sin = jnp.concatenate([jnp.zeros((lc, dk // 2), F32), jnp.sin(ang)], axis=0)
    return cos, sin


def kernel(x, c, ctx, c_ctx, ada_w, ada_b, norm_g, rk_mix, rk_w_in, rk_w0, rk_w1, rk_w2, rk_a0, rk_a1,
           rk_a2, rk_k_k, rk_k_a, rk_r_k, rk_ln_g, rk_ln_b, rk_w_out, rt_w_in, rt_decay_logit, rt_gn_g,
           rt_w_out, final_g):
    b, l, d = x.shape
    lc = ctx.shape[1]
    lt = lc + l
    assert lc == ROW_TILE and l % ROW_TILE == 0 and ROW_TILE % GRID_W == 0
    assert d % min(WKV_LANES, d) == 0 and d % WKV_GROUP == 0 and (b * lt) % MM_TILE == 0
    assert ada_w.shape[0] == 2 and rk_w_in.shape[0] == 1 and rt_w_in.shape[0] == 1
    dk = d // RET_HEADS
    dv = rt_w_out.shape[1] // RET_HEADS

    n_mod = -(-(b + 1) // 8) * 8
    cv = jnp.concatenate([c, c_ctx[None], jnp.zeros((n_mod - b - 1, d), F32)], axis=0)
    mod = _adaln(cv, ada_w, ada_b).reshape(2, n_mod, 3, d)

    xm, lw, a = _rk_pro(x, ctx, mod[0], norm_g[0], rk_mix[0], rk_w1[0].astype(BF16),
                        rk_w2[0].astype(BF16), rk_a1[0].astype(BF16), rk_a2[0].astype(BF16),
                        rk_w0[0], rk_a0[0])
    proj = _rk_proj(xm.reshape(4, b * lt, d), rk_w_in[0].astype(BF16)).reshape(4, b, lt, d)
    k_k = rk_k_k[0].reshape(1, d)
    k_a = rk_k_a[0].reshape(1, d)
    n_ctx_blocks = lc // ROW_TILE
    o_f = _wkv(proj, lw, a, k_k, k_a, 0, n_ctx_blocks)
    o_sum = _wkv(proj, lw, a, k_k, k_a, 1, n_ctx_blocks, prev=o_f)
    x1, ctx1, h1 = _rk_out(o_sum, proj, a, x, ctx, mod[0], mod[1], norm_g[1], k_a,
                           rk_r_k[0].reshape(1, d), rk_ln_g[0].reshape(1, d), rk_ln_b[0].reshape(1, d),
                           rk_w_out[0].astype(BF16))
    del ctx1

    cos, sin = _rope_tables(l, lc, dk)
    qkvg = _rt_proj(h1.reshape(b * lt, d), rt_w_in[0].astype(BF16), jnp.tile(cos, (b, 1)),
                    jnp.tile(sin, (b, 1)), dk)
    y = _ret(qkvg.reshape(b, lt, -1), rt_decay_logit[0], rt_gn_g[0], lc // RET_CHUNK, dk, dv)
    return _rt_out(y, qkvg.reshape(b, lt, -1), x1, mod[1], final_g, rt_w_out[0].astype(BF16),
                   lc // ROW_TILE)
```

```python
import functools
import math

import jax
import jax.numpy as jnp
from jax import lax
from jax.experimental import pallas as pl
from jax.experimental.pallas import tpu as pltpu

F32 = jnp.float32
BF16 = jnp.bfloat16

GRID_W = 64
EPS = 1e-6
RWKV_HEAD = 64
RWKV_GN_EPS = 64e-5
RET_HEADS = 8
ROPE_BASE = 10000.0

LANE = 128
MXU_DIM = 256
VMEM_LIMIT = 56 * 1024 * 1024

ROW_TILE = 256
OUT_TILE = 128
RK_OUT_SPLITS = 2
MM_TILE = 512
RT_PROJ_TILE = 1024
WKV_CHUNK = 64
WKV_GROUP = MXU_DIM
WKV_STAGGER = 3
WKV_LANES = 2048
RET_CHUNK = 256
ACT_DTYPE = BF16


def _cparams(sem, vmem=VMEM_LIMIT):
    return pltpu.CompilerParams(dimension_semantics=sem, vmem_limit_bytes=vmem)


def _mm(a, b):
    return jnp.dot(a.astype(BF16), b.astype(BF16), preferred_element_type=F32)


def _mm_nt(a, b):
    return lax.dot_general(a.astype(BF16), b.astype(BF16), (((1,), (1,)), ((), ())),
                           preferred_element_type=F32)


def _mm_tn(a, b):
    return lax.dot_general(a.astype(BF16), b.astype(BF16), (((0,), (0,)), ((), ())),
                           preferred_element_type=F32)


def _split2(x):
    hi = x.astype(BF16)
    lo = (x - hi.astype(F32)).astype(BF16)
    return hi, lo


def _split3(x):
    hi = x.astype(BF16)
    r = x - hi.astype(F32)
    mid = r.astype(BF16)
    lo = (r - mid.astype(F32)).astype(BF16)
    return hi, mid, lo


def _dot(a, b, dims):
    return lax.dot_general(a, b, dims, preferred_element_type=F32)


_NN = (((1,), (0,)), ((), ()))
_NT = (((1,), (1,)), ((), ()))
_TN = (((0,), (0,)), ((), ()))


def _mm3(a, b, dims=_NN):
    a1, a2 = _split2(a)
    b1, b2 = _split2(b)
    return _dot(a1, b1, dims) + (_dot(a1, b2, dims) + _dot(a2, b1, dims))


def _mm_exact_lhs(a_bf16, b):
    b1, b2, b3 = _split3(b)
    return _dot(a_bf16, b1, _NN) + (_dot(a_bf16, b2, _NN) + _dot(a_bf16, b3, _NN))


def _mm_exact_rhs(a, b_bf16):
    a1, a2, a3 = _split3(a)
    return _dot(a1, b_bf16, _NN) + (_dot(a2, b_bf16, _NN) + _dot(a3, b_bf16, _NN))


def _sigmoid(x):
    return 0.5 * jnp.tanh(0.5 * x) + 0.5


def _silu(x):
    h = 0.5 * x
    return h + h * jnp.tanh(h)


def _norm_mod(xb, g, mod):
    y = xb * lax.rsqrt(jnp.mean(xb * xb, axis=-1, keepdims=True) + EPS) * g
    return y * (1.0 + mod[1:2]) + mod[0:1]


def _adaln_kernel(cv_ref, w_ref, b_ref, o_ref):
    s = _silu(cv_ref[...])
    o_ref[...] = _mm3(s, w_ref[...]) + b_ref[...]


def _adaln(cv, ada_w, ada_b):
    depth, d, d3 = ada_w.shape
    rows = cv.shape[0]
    tn = math.gcd(d3, 6 * LANE)
    return pl.pallas_call(
        _adaln_kernel,
        out_shape=jax.ShapeDtypeStruct((depth, rows, d3), F32),
        grid=(depth, d3 // tn),
        in_specs=[
            pl.BlockSpec((rows, d), lambda i, n: (0, 0)),
            pl.BlockSpec((None, d, tn), lambda i, n: (i, 0, n)),
            pl.BlockSpec((None, 1, tn), lambda i, n: (i, 0, n)),
        ],
        out_specs=pl.BlockSpec((None, rows, tn), lambda i, n: (i, 0, n)),
        compiler_params=_cparams(("parallel", "parallel")),
        name="adaln",
    )(cv, ada_w, ada_b.reshape(depth, 1, d3))


def _rk_pro_kernel(ctx_ref, x_ref, xp_ref, xn_ref, modc_ref, modl_ref, g_ref, mix_ref,
                   w1_ref, w2_ref, a1_ref, a2_ref, w0_ref, a0_ref, o_ref, lw_ref, a_ref,
                   *, n_lat_tiles):
    j = pl.program_id(1)
    g = g_ref[...]
    mix = mix_ref[...]
    t, d = x_ref.shape
    row = lax.broadcasted_iota(jnp.int32, (t, 1), 0)
    n_proj = o_ref.shape[0]

    def emit(h, hs):
        delta = hs - h
        for p in range(n_proj):
            o_ref[p] = (h + delta * mix[p:p + 1]).astype(BF16)
        xw = (h + delta * mix[n_proj:n_proj + 1]).astype(BF16)
        xa = (h + delta * mix[n_proj + 1:n_proj + 2]).astype(BF16)
        for n in range(2):
            hw = jnp.tanh(jnp.dot(xw, w1_ref[n], preferred_element_type=F32))
            ww = w0_ref[n:n + 1, :] + _mm(hw, w2_ref[n])
            half_c = -0.5 * math.exp(-0.5)
            lw_ref[n] = half_c * jnp.tanh(0.5 * ww) + half_c
            ha = jnp.dot(xa, a1_ref[n], preferred_element_type=F32)
            a_ref[n] = _sigmoid(a0_ref[n:n + 1, :] + _mm(ha, a2_ref[n])).astype(a_ref.dtype)

    @pl.when(j == 0)
    def _():
        h = _norm_mod(ctx_ref[...], g, modc_ref[...])
        half = d // 2
        prev = jnp.where(row == 0, 0.0, pltpu.roll(h[:, :half], 1, 0))
        nxt = jnp.where(row == t - 1, 0.0, pltpu.roll(h[:, half:], t - 1, 0))
        emit(h, jnp.concatenate([prev, nxt], axis=-1))

    @pl.when(j > 0)
    def _():
        mod = modl_ref[...]
        h = _norm_mod(x_ref[...], g, mod)
        hp = _norm_mod(xp_ref[...], g, mod)
        hn = _norm_mod(xn_ref[...], g, mod)
        q = d // 4
        col = row % GRID_W
        left = jnp.where(col == 0, 0.0, pltpu.roll(h[:, :q], 1, 0))
        right = jnp.where(col == GRID_W - 1, 0.0, pltpu.roll(h[:, q:2 * q], t - 1, 0))
        up = jnp.concatenate([hp[:, 2 * q:3 * q], h[:t - GRID_W, 2 * q:3 * q]], axis=0)
        up = jnp.where((j == 1) & (row < GRID_W), 0.0, up)
        down = jnp.concatenate([h[GRID_W:, 3 * q:], hn[:, 3 * q:]], axis=0)
        down = jnp.where((j == n_lat_tiles) & (row >= t - GRID_W), 0.0, down)
        emit(h, jnp.concatenate([left, right, up, down], axis=-1))


def _rk_pro(x, ctx, mod0, norm_g0, mix, w1, w2, a1, a2, w0, a0):
    b, l, d = x.shape
    t = ROW_TILE
    n_lat = l // t
    sub = t // GRID_W
    n_rows = l // GRID_W
    xr = x.reshape(b, n_rows, GRID_W, d)
    nmix = mix.shape[0]
    n_proj = nmix - 2
    r = w1.shape[-1]
    lt = ctx.shape[1] + l
    full = lambda shape: pl.BlockSpec(shape, lambda bi, j: (0,) * len(shape))
    tok2 = pl.BlockSpec((2, None, t, d), lambda bi, j: (0, bi, j, 0))
    return pl.pallas_call(
        functools.partial(_rk_pro_kernel, n_lat_tiles=n_lat),
        out_shape=(jax.ShapeDtypeStruct((n_proj, b, lt, d), BF16),
                   jax.ShapeDtypeStruct((2, b, lt, d), F32),
                   jax.ShapeDtypeStruct((2, b, lt, d), ACT_DTYPE)),
        grid=(b, n_lat + 1),
        in_specs=[
            pl.BlockSpec((None, t, d), lambda bi, j: (bi, 0, 0)),
            pl.BlockSpec((None, t, d), lambda bi, j: (bi, jnp.maximum(j - 1, 0), 0)),
            pl.BlockSpec((None, None, GRID_W, d),
                         lambda bi, j: (bi, jnp.maximum((j - 1) * sub - 1, 0), 0, 0)),
            pl.BlockSpec((None, None, GRID_W, d),
                         lambda bi, j: (bi, jnp.minimum(jnp.maximum(j, 1) * sub, n_rows - 1), 0, 0)),
            pl.BlockSpec((None, 3, d), lambda bi, j: (b, 0, 0)),
            pl.BlockSpec((None, 3, d), lambda bi, j: (bi, 0, 0)),
            pl.BlockSpec((1, d), lambda bi, j: (0, 0)),
            pl.BlockSpec((nmix, d), lambda bi, j: (0, 0)),
            full((2, d, r)), full((2, r, d)), full((2, d, r)), full((2, r, d)),
            full((2, d)), full((2, d)),
        ],
        out_specs=(pl.BlockSpec((n_proj, None, t, d), lambda bi, j: (0, bi, j, 0)), tok2, tok2),
        compiler_params=_cparams(("parallel", "parallel")),
        name="rk_pro",
    )(ctx, x, xr, xr, mod0, mod0, norm_g0.reshape(1, d), mix, w1, w2, a1, a2, w0, a0)


def _rk_proj_kernel(x_ref, w_ref, o_ref):
    o_ref[...] = jnp.dot(x_ref[...], w_ref[...], preferred_element_type=F32).astype(o_ref.dtype)


def _rk_proj(xm, w_in):
    nproj, d, _ = w_in.shape
    m = xm.shape[1]
    tm = RT_PROJ_TILE if m % RT_PROJ_TILE == 0 else MM_TILE
    return pl.pallas_call(
        _rk_proj_kernel,
        out_shape=jax.ShapeDtypeStruct((nproj, m, d), ACT_DTYPE),
        grid=(nproj, m // tm),
        in_specs=[
            pl.BlockSpec((None, tm, d), lambda p, i: (p, i, 0)),
            pl.BlockSpec((None, d, d), lambda p, i: (p, 0, 0)),
        ],
        out_specs=pl.BlockSpec((None, tm, d), lambda p, i: (p, i, 0)),
        compiler_params=_cparams(("parallel", "parallel")),
        name="rk_proj",
    )(xm, w_in)


def _wkv_kernel(r_ref, k_ref, v_ref, lw_ref, a_ref, kk_ref, ka_ref, *rest, reverse, chunks_per_block):
    prev_ref = rest[0] if len(rest) == 3 else None
    o_ref, t_ref = rest[-2:]
    c = WKV_CHUNK
    gw = WKV_GROUP
    nq = r_ref.shape[-1] // gw
    nh = gw // RWKV_HEAD
    hb = gw
    s = pl.program_id(2)

    @pl.when(s == 0)
    def _():
        t_ref[...] = jnp.zeros_like(t_ref)

    row_id = lax.broadcasted_iota(jnp.int32, (c, 1), 0)

    def running_sum(x):
        sh = 1
        while sh < c:
            if reverse:
                x = x + jnp.where(row_id < c - sh, pltpu.roll(x, c - sh, 0), 0.0)
            else:
                x = x + jnp.where(row_id >= sh, pltpu.roll(x, sh, 0), 0.0)
            sh *= 2
        return x

    rw = lax.broadcasted_iota(jnp.int32, (c, nh * c), 0)
    cw = lax.broadcasted_iota(jnp.int32, (c, nh * c), 1) % c
    strict_w = (cw > rw) if reverse else (cw < rw)
    incl_w = strict_w | (cw == rw)
    eye_w = (cw == rw).astype(F32)
    rs = lax.broadcasted_iota(jnp.int32, (nh * c, nh * c), 0)
    cs = lax.broadcasted_iota(jnp.int32, (nh * c, nh * c), 1)
    bd_mask = ((rs // c) == (cs // c)).astype(BF16)
    rl = lax.broadcasted_iota(jnp.int32, (hb, hb), 0)
    cl = lax.broadcasted_iota(jnp.int32, (hb, hb), 1)
    head_bd = (rl // RWKV_HEAD) == (cl // RWKV_HEAD)
    ones_bd = head_bd.astype(BF16)
    lane_head = lax.broadcasted_iota(jnp.int32, (1, hb), 1) // RWKV_HEAD
    lane_masks = [(lane_head == h).astype(BF16) for h in range(nh)]
    end_row = 0 if reverse else c - 1

    def stack(x):
        xb = x.astype(BF16)
        return jnp.concatenate([xb * m for m in lane_masks], axis=0)

    def block_diag(xw):
        return jnp.concatenate([xw.astype(BF16)] * nh, axis=0) * bd_mask

    def prepass(cidx, q, out):
        rows = pl.ds(cidx * c, c)
        lanes = pl.ds(q * gw, gw)
        r = r_ref[rows, lanes].astype(F32)
        k = k_ref[rows, lanes].astype(F32)
        v = v_ref[rows, lanes].astype(F32)
        lw = lw_ref[rows, lanes]
        a = a_ref[rows, lanes].astype(F32)
        kap = ka_ref[:, lanes]

        b = running_sum(lw)
        kk = k * kk_ref[:, lanes]
        ssq = ssq_blocks[q][cidx * c:(cidx + 1) * c]
        b_end = b[end_row:end_row + 1, :]
        e_pos = jnp.exp(b)
        e_neg = jnp.exp(-b)
        e_prev = jnp.exp(b - lw)
        e_end = jnp.exp(b_end - b)
        p_end = jnp.exp(b_end)

        kappa = kk / jnp.maximum(jnp.sqrt(ssq), 1e-12)
        kd = k * (1.0 + (a - 1.0) * kap)
        beta = kappa * a

        kt = kappa * e_prev
        rt = r * e_pos
        bt_s = stack(beta * e_neg)
        kn_s = stack(kd * e_neg)
        v_s = stack(v)
        bbar = beta * e_end
        kbar = kd * e_end

        lhs = jnp.concatenate([kt, rt], axis=0)
        ab = _mm_nt(lhs, bt_s)
        ak = _mm_nt(lhs, kn_s)
        yield
        a_ab = jnp.where(strict_w, ab[:c], 0.0)
        a_rb = jnp.where(incl_w, ab[c:], 0.0)
        a_ak = jnp.where(strict_w, ak[:c], 0.0)
        a_rk = jnp.where(incl_w, ak[c:], 0.0)

        pw = -a_ab
        minv = eye_w + pw
        pw = _mm(pw, block_diag(pw))
        intra_v = _mm(jnp.concatenate([a_ak, a_rk], axis=0), v_s)
        av = intra_v[:c]
        ark_v = intra_v[c:]
        yield
        n_sq = int(math.log2(c)) - 1
        for i in range(n_sq):
            both = _mm(jnp.concatenate([minv, pw], axis=0) if i < n_sq - 1 else minv, block_diag(pw))
            minv = minv + both[:c]
            pw = both[c:]
            yield

        keys = jnp.concatenate([kbar, bbar], axis=0).astype(BF16)
        out[(cidx, q)] = (lhs.astype(BF16), minv.astype(BF16), a_rb.astype(BF16), av, ark_v,
                          v.astype(BF16), keys, p_end)

    order = list(range(chunks_per_block))
    if reverse:
        order.reverse()
    factors = {}
    ssq_blocks = []
    for q in range(nq):
        kk_blk = k_ref[:, pl.ds(q * gw, gw)].astype(F32) * kk_ref[:, pl.ds(q * gw, gw)]
        ssq_blocks.append(_mm(kk_blk * kk_blk, ones_bd))
    def recurrence():
        for cidx in order:
            while any((cidx, q) not in factors for q in range(nq)):
                yield
            rows = pl.ds(cidx * c, c)
            fac = [factors[(cidx, q)] for q in range(nq)]
            reads = [_dot(f[0], t_ref[q].astype(BF16), _NT) for q, f in enumerate(fac)]
            yield
            us = [-_dot(f[1], stack(reads[q][:c] + f[3]), _NN) for q, f in enumerate(fac)]
            yield
            for q, f in enumerate(fac):
                _, _, a_rb, _, ark_v, vb, keys, p_end = f
                o = reads[q][c:] + ark_v + _dot(a_rb, stack(us[q]), _NN)
                if prev_ref is not None:
                    o = o + prev_ref[rows, pl.ds(q * gw, gw)].astype(F32)
                o_ref[rows, pl.ds(q * gw, gw)] = o.astype(o_ref.dtype)
                vu = jnp.concatenate([vb, us[q].astype(BF16)], axis=0)
                t_ref[q] = t_ref[q] * p_end + jnp.where(head_bd, _dot(vu, keys, _TN), 0.0)
            yield

    chains = [prepass(cidx, q, factors) for cidx in order for q in range(nq)]
    start = [(i // nq) * WKV_STAGGER for i in range(len(chains))]
    live = list(range(len(chains)))
    scan = recurrence()
    rnd = 0
    while scan is not None:
        live = [i for i in live if start[i] > rnd or next(chains[i], "done") != "done"]
        if next(scan, "done") == "done":
            scan = None
        rnd += 1


def _wkv(proj, lw, a, k_k, k_a, direction, n_ctx_blocks, prev=None):
    _, b, lt, d = proj.shape
    rb = ROW_TILE
    hb = min(WKV_LANES, d)
    nblk = lt // rb
    reverse = direction == 1

    if reverse:
        def blk(s):
            return jnp.where(s < n_ctx_blocks, n_ctx_blocks - 1 - s, nblk - 1 - (s - n_ctx_blocks))
    else:
        def blk(s):
            return s

    def tok(p):
        return pl.BlockSpec((None, None, rb, hb), lambda bi, hi, s: (p, bi, blk(s), hi))

    out_spec = pl.BlockSpec((None, rb, hb), lambda bi, hi, s: (bi, blk(s), hi))
    extra = () if prev is None else (prev,)
    return pl.pallas_call(
        functools.partial(_wkv_kernel, reverse=reverse, chunks_per_block=rb // WKV_CHUNK),
        out_shape=jax.ShapeDtypeStruct((b, lt, d), ACT_DTYPE),
        grid=(b, d // hb, nblk),
        in_specs=[tok(0), tok(1), tok(2), tok(direction), tok(direction),
                  pl.BlockSpec((1, hb), lambda bi, hi, s: (0, hi)),
                  pl.BlockSpec((1, hb), lambda bi, hi, s: (0, hi))] + [out_spec] * len(extra),
        out_specs=out_spec,
        scratch_shapes=[pltpu.VMEM((hb // WKV_GROUP, WKV_GROUP, WKV_GROUP), F32)],
        compiler_params=_cparams(("parallel", "parallel", "arbitrary")),
        name="wkv_bwd" if reverse else "wkv_fwd",
    )(proj, proj, proj, lw, a, k_k, k_a, *extra)


def _head_sums(x, ones_bd):
    w = ones_bd.shape[0]
    return jnp.concatenate(
        [_mm(x[:, i:i + w], ones_bd) for i in range(0, x.shape[-1], w)], axis=-1)


def _rk_out_kernel(o_ref, r_ref, k_ref, v_ref, g_ref, a_ref, x_ref, ctx_ref,
                   modl_ref, modc_ref, modl1_ref, modc1_ref, ng1_ref,
                   ka_ref, rk_ref, lng_ref, lnb_ref, w_ref,
                   x1_ref, h1_ref, *, n_ctx_tiles):
    is_ctx = pl.program_id(1) < n_ctx_tiles
    gate = jnp.where(is_ctx, modc_ref[2:3, :], modl_ref[2:3, :])
    mod1 = jnp.where(is_ctx, modc1_ref[...], modl1_ref[...])
    ng1 = ng1_ref[...]
    hw = min(MXU_DIM, r_ref.shape[-1])
    rl = lax.broadcasted_iota(jnp.int32, (hw, hw), 0)
    cl = lax.broadcasted_iota(jnp.int32, (hw, hw), 1)
    same_head = (rl // RWKV_HEAD) == (cl // RWKV_HEAD)
    ones_bd = same_head.astype(BF16)
    mean_bd = jnp.where(same_head, 1.0 / RWKV_HEAD, 0.0).astype(BF16)

    groups = [pl.ds(i, hw) for i in range(0, r_ref.shape[-1], hw)]
    f32 = lambda ref, *idx: ref[idx].astype(F32)
    sub = min(OUT_TILE, o_ref.shape[0])
    for r0 in range(0, o_ref.shape[0], sub):
        rows = pl.ds(r0, sub)
        mu = [_mm(o_ref[rows, gs], mean_bd) for gs in groups]
        rkk = []
        for gs in groups:
            ka = ka_ref[:, gs]
            rk = rk_ref[:, gs]
            a_sum = (a_ref[0, rows, gs] + a_ref[1, rows, gs]).astype(F32)
            scale = (2.0 - 2.0 * ka) * rk + a_sum * (ka * rk)
            rkk.append(_mm((r_ref[rows, gs] * k_ref[rows, gs]).astype(F32) * scale, ones_bd))
        oc = [f32(o_ref, rows, gs) - mg for gs, mg in zip(groups, mu)]
        var = [_mm(cg * cg, mean_bd) for cg in oc]
        out = None
        per = max(1, len(groups) // RK_OUT_SPLITS)
        for g0 in range(0, len(groups), per):
            ys = []
            for gi in range(g0, min(g0 + per, len(groups))):
                gs = groups[gi]
                on = oc[gi] * lax.rsqrt(var[gi] + RWKV_GN_EPS) * lng_ref[:, gs] + lnb_ref[:, gs]
                y = (on + rkk[gi] * f32(v_ref, rows, gs)) * _silu(f32(g_ref, rows, gs))
                ys.append(y.astype(BF16))
            part = _dot(jnp.concatenate(ys, axis=-1), w_ref[pl.ds(g0 * hw, len(ys) * hw), :], _NN)
            out = part if out is None else out + part
        new = jnp.where(is_ctx, ctx_ref[rows, :], x_ref[rows, :]) + gate * out
        x1_ref[rows, :] = new
        h1_ref[rows, :] = _norm_mod(new, ng1, mod1).astype(BF16)


def _rk_out(o_sum, proj, a, x, ctx, mod0, mod1, norm_g1, k_a, r_k, ln_g, ln_b, w_out):
    b, l, d = x.shape
    lc = ctx.shape[1]
    lt = lc + l
    t = ROW_TILE
    nct = lc // t
    tok = pl.BlockSpec((None, t, d), lambda bi, j: (bi, j, 0))
    ptok = lambda p: pl.BlockSpec((None, None, t, d), lambda bi, j: (p, bi, j, 0))
    vec = pl.BlockSpec((1, d), lambda bi, j: (0, 0))
    lat_blk = lambda bi, j: (bi, jnp.maximum(j - nct, 0), 0)
    mod_l = pl.BlockSpec((None, 3, d), lambda bi, j: (bi, 0, 0))
    mod_c = pl.BlockSpec((None, 3, d), lambda bi, j: (b, 0, 0))
    return pl.pallas_call(
        functools.partial(_rk_out_kernel, n_ctx_tiles=nct),
        out_shape=(jax.ShapeDtypeStruct((b, l, d), F32),
                   jax.ShapeDtypeStruct((b, lt, d), BF16)),
        grid=(b, lt // t),
        in_specs=[tok, ptok(0), ptok(1), ptok(2), ptok(3),
                  pl.BlockSpec((2, None, t, d), lambda bi, j: (0, bi, j, 0)),
                  pl.BlockSpec((None, t, d), lat_blk),
                  pl.BlockSpec((None, lc, d), lambda bi, j: (bi, 0, 0)),
                  mod_l, mod_c, mod_l, mod_c, vec, vec, vec, vec, vec,
                  pl.BlockSpec((d, d), lambda bi, j: (0, 0))],
        out_specs=(pl.BlockSpec((None, t, d), lat_blk),
                   pl.BlockSpec((None, t, d), lambda bi, j: (bi, j, 0))),
        compiler_params=_cparams(("parallel", "arbitrary")),
        name="rk_out",
    )(o_sum, proj, proj, proj, proj, a, x, ctx, mod0, mod0, mod1, mod1,
      norm_g1.reshape(1, d), k_a, r_k, ln_g, ln_b, w_out)


def _rt_proj_kernel(h_ref, w_ref, cos_ref, sin_ref, o_ref, *, dk):
    n = pl.program_id(0)
    tn = w_ref.shape[-1]
    half = dk // 2
    sub = min(ROW_TILE, h_ref.shape[0])
    chunks = [pl.ds(i, sub) for i in range(0, h_ref.shape[0], sub)]

    def project(rows):
        return jnp.dot(h_ref[rows, :], w_ref[...], preferred_element_type=F32)

    def rotate(scale):
        for rows in chunks:
            acc = project(rows)
            cos = cos_ref[rows, :]
            sin = sin_ref[rows, :]
            parts = []
            for h in range(tn // dk):
                x1 = acc[:, h * dk:h * dk + half]
                x2 = acc[:, h * dk + half:(h + 1) * dk]
                parts.append((x1 * cos - x2 * sin) * scale)
                parts.append((x1 * sin + x2 * cos) * scale)
            o_ref[rows, :] = jnp.concatenate(parts, axis=-1).astype(BF16)

    @pl.when(n == 0)
    def _():
        rotate(1.0)

    @pl.when(n == 1)
    def _():
        rotate(dk ** -0.5)

    @pl.when(n > 1)
    def _():
        for rows in chunks:
            o_ref[rows, :] = project(rows).astype(BF16)


def _rt_proj(h1, w_in, cos, sin, dk):
    m, d = h1.shape
    n_out = w_in.shape[1]
    tn = d
    tm = RT_PROJ_TILE if m % RT_PROJ_TILE == 0 else MM_TILE
    half = dk // 2
    return pl.pallas_call(
        functools.partial(_rt_proj_kernel, dk=dk),
        out_shape=jax.ShapeDtypeStruct((m, n_out), BF16),
        grid=(n_out // tn, m // tm),
        in_specs=[
            pl.BlockSpec((tm, d), lambda n, i: (i, 0)),
            pl.BlockSpec((d, tn), lambda n, i: (0, n)),
            pl.BlockSpec((tm, half), lambda n, i: (i, 0)),
            pl.BlockSpec((tm, half), lambda n, i: (i, 0)),
        ],
        out_specs=pl.BlockSpec((tm, tn), lambda n, i: (i, n)),
        compiler_params=_cparams(("parallel", "parallel")),
        name="rt_proj",
    )(h1, w_in, cos, sin)


def _log_sigmoid(x):
    return jnp.minimum(x, 0.0) - jnp.log1p(jnp.exp(-jnp.abs(x)))


def _ret_bstate_kernel(lg_ref, k_ref, v_ref, rb_ref, r_ref, *, dk, dv):
    c = k_ref.shape[0]
    s = pl.program_id(1)

    @pl.when(s == 0)
    def _():
        r_ref[...] = jnp.zeros_like(r_ref)

    pos = lax.broadcasted_iota(jnp.int32, (c, 1), 0).astype(F32)
    for h in range(RET_HEADS):
        lgb = _log_sigmoid(lg_ref[1, h])[:, 0:1]
        k = k_ref[:, h * dk:(h + 1) * dk]
        v = v_ref[:, h * dv:(h + 1) * dv]
        r0 = r_ref[h]
        rb_ref[h] = r0.astype(BF16)
        kd = (k.astype(F32) * jnp.exp(pos * lgb)).astype(BF16)
        r_ref[h] = r0 * jnp.exp(float(c) * lgb) + _dot(kd, v, _TN)


def _ret_main_kernel(lg_ref, q_ref, k_ref, v_ref, rb_ref, gng_ref, y_ref, rf_ref, dm_ref, dvec_ref,
                     *, dk, dv, n_ctx_blocks):
    c = q_ref.shape[0]
    s = pl.program_id(1)

    @pl.when(s == 0)
    def _():
        rf_ref[...] = jnp.zeros_like(rf_ref)
        diff = (lax.broadcasted_iota(jnp.int32, (c, c), 0)
                - lax.broadcasted_iota(jnp.int32, (c, c), 1)).astype(F32)
        pos = lax.broadcasted_iota(jnp.int32, (c, LANE), 0).astype(F32)
        for h in range(RET_HEADS):
            lgf = _log_sigmoid(lg_ref[0, h])
            lgb = _log_sigmoid(lg_ref[1, h])
            dm_ref[h] = jnp.where(diff >= 0, jnp.exp(jnp.maximum(diff, 0.0) * lgf[:, 0:1]),
                                  jnp.exp(jnp.maximum(-diff, 0.0) * lgb[:, 0:1]))
            dvec_ref[h, 0] = jnp.exp((pos + 1.0) * lgf)
            dvec_ref[h, 1] = jnp.exp((float(c) - pos) * lgb)
            dvec_ref[h, 2] = jnp.exp((float(c) - 1.0 - pos) * lgf)

    def lanes(x, width):
        return x[:, :width] if width <= LANE else jnp.concatenate([x] * (width // LANE), axis=1)

    @pl.when(s >= n_ctx_blocks)
    def _():
        scores = []
        for h in range(RET_HEADS):
            q = q_ref[:, h * dk:(h + 1) * dk]
            k = k_ref[:, h * dk:(h + 1) * dk]
            scores.append(_dot(q, k, _NT))
        outs = []
        for h in range(RET_HEADS):
            q = q_ref[:, h * dk:(h + 1) * dk]
            sc = (scores[h] * dm_ref[h]).astype(BF16)
            lhs = jnp.concatenate([sc, q * lanes(dvec_ref[h, 0], dk).astype(BF16),
                                   q * lanes(dvec_ref[h, 1], dk).astype(BF16)], axis=1)
            rhs = jnp.concatenate([v_ref[:, h * dv:(h + 1) * dv], rf_ref[h].astype(BF16), rb_ref[h]],
                                  axis=0)
            outs.append(_dot(lhs, rhs, _NN))
        for h in range(RET_HEADS):
            o = outs[h]
            on = o * lax.rsqrt(jnp.mean(o * o, axis=-1, keepdims=True) + EPS)
            y_ref[:, h * dv:(h + 1) * dv] = (on * gng_ref[:, h * dv:(h + 1) * dv]).astype(BF16)

    for h in range(RET_HEADS):
        k = k_ref[:, h * dk:(h + 1) * dk]
        v = v_ref[:, h * dv:(h + 1) * dv]
        kd = (k.astype(F32) * lanes(dvec_ref[h, 2], dk)).astype(BF16)
        gamma_c = jnp.exp(float(c) * _log_sigmoid(lg_ref[0, h]))
        rf_ref[h] = rf_ref[h] * lanes(gamma_c, dv) + _dot(kd, v, _TN)


def _ret(qkvg, decay_logit, gn_g, n_ctx_blocks, dk, dv):
    b, lt, _ = qkvg.shape
    c = RET_CHUNK
    nblk = lt // c
    nh = RET_HEADS
    d = nh * dk
    vd = nh * dv
    assert (2 * d) % vd == 0
    voff = (2 * d) // vd
    lg = jnp.broadcast_to(decay_logit.astype(F32)[:, :, None, None], (2, nh, 1, LANE))
    lg_spec = pl.BlockSpec((2, nh, 1, LANE), lambda bi, s: (0, 0, 0, 0))

    def bwd_blk(s):
        return jnp.where(s < n_ctx_blocks, n_ctx_blocks - 1 - s, nblk - 1 - (s - n_ctx_blocks))

    r_b = pl.pallas_call(
        functools.partial(_ret_bstate_kernel, dk=dk, dv=dv),
        out_shape=jax.ShapeDtypeStruct((b, nblk, nh, dk, dv), BF16),
        grid=(b, nblk),
        in_specs=[
            lg_spec,
            pl.BlockSpec((None, c, d), lambda bi, s: (bi, bwd_blk(s), 1)),
            pl.BlockSpec((None, c, vd), lambda bi, s: (bi, bwd_blk(s), voff)),
        ],
        out_specs=pl.BlockSpec((None, None, nh, dk, dv), lambda bi, s: (bi, bwd_blk(s), 0, 0, 0)),
        scratch_shapes=[pltpu.VMEM((nh, dk, dv), F32)],
        compiler_params=_cparams(("parallel", "arbitrary")),
        name="ret_bstate",
    )(lg, qkvg, qkvg)

    lat = lambda bi, s: (bi, jnp.maximum(s - n_ctx_blocks, 0), 0)
    return pl.pallas_call(
        functools.partial(_ret_main_kernel, dk=dk, dv=dv, n_ctx_blocks=n_ctx_blocks),
        out_shape=jax.ShapeDtypeStruct((b, lt - n_ctx_blocks * c, vd), BF16),
        grid=(b, nblk),
        in_specs=[
            lg_spec,
            pl.BlockSpec((None, c, d), lambda bi, s: (bi, s, 0)),
            pl.BlockSpec((None, c, d), lambda bi, s: (bi, s, 1)),
            pl.BlockSpec((None, c, vd), lambda bi, s: (bi, s, voff)),
            pl.BlockSpec((None, None, nh, dk, dv), lambda bi, s: (bi, s, 0, 0, 0)),
            pl.BlockSpec((1, vd), lambda bi, s: (0, 0)),
        ],
        out_specs=pl.BlockSpec((None, c, vd), lat),
        scratch_shapes=[pltpu.VMEM((nh, dk, dv), F32), pltpu.VMEM((nh, c, c), F32),
                        pltpu.VMEM((nh, 3, c, LANE), F32)],
        compiler_params=_cparams(("parallel", "arbitrary")),
        name="ret_main",
    )(lg, qkvg, qkvg, qkvg, r_b, gn_g.reshape(1, vd))


def _rt_out_kernel(y_ref, g_ref, x_ref, mod_ref, fg_ref, w_ref, o_ref):
    gated = y_ref[...].astype(F32) * _silu(g_ref[...].astype(F32))
    new = x_ref[...] + mod_ref[2:3, :] * _mm(gated, w_ref[...])
    o_ref[...] = new * lax.rsqrt(jnp.mean(new * new, axis=-1, keepdims=True) + EPS) * fg_ref[...]


def _rt_out(y, qkvg, x1, mod1, final_g, w_out, n_ctx_tiles):
    b, l, d = x1.shape
    vd = w_out.shape[0]
    t = ROW_TILE
    goff = (qkvg.shape[-1] - vd) // vd
    return pl.pallas_call(
        _rt_out_kernel,
        out_shape=jax.ShapeDtypeStruct((b, l, d), F32),
        grid=(b, l // t),
        in_specs=[
            pl.BlockSpec((None, t, vd), lambda bi, j: (bi, j, 0)),
            pl.BlockSpec((None, t, vd), lambda bi, j: (bi, j + n_ctx_tiles, goff)),
            pl.BlockSpec((None, t, d), lambda bi, j: (bi, j, 0)),
            pl.BlockSpec((None, 3, d), lambda bi, j: (bi, 0, 0)),
            pl.BlockSpec((1, d), lambda bi, j: (0, 0)),
            pl.BlockSpec((vd, d), lambda bi, j: (0, 0)),
        ],
        out_specs=pl.BlockSpec((None, t, d), lambda bi, j: (bi, j, 0)),
        compiler_params=_cparams(("parallel", "parallel")),
        name="rt_out",
    )(y, qkvg, x1, mod1, final_g.reshape(1, d), w_out)


def _rope_tables(l, lc, dk):
    t = jnp.arange(l)
    row = (t // GRID_W).astype(F32)
    col = (t % GRID_W).astype(F32)
    nf = dk // 4
    inv = ROPE_BASE ** (-jnp.arange(nf, dtype=F32) / nf)
    ang = jnp.concatenate([row[:, None] * inv, col[:, None] * inv], axis=-1)
    cos = jnp.concatenate([jnp.ones((lc, dk // 2), F32), jnp.cos(ang)], axis=0)
    sin = jnp.concatenate([jnp.zeros((lc, dk // 2), F32), jnp.sin(ang)], axis=0)
    return cos, sin


def kernel(x, c, ctx, c_ctx, ada_w, ada_b, norm_g, rk_mix, rk_w_in, rk_w0, rk_w1, rk_w2, rk_a0, rk_a1,
           rk_a2, rk_k_k, rk_k_a, rk_r_k, rk_ln_g, rk_ln_b, rk_w_out, rt_w_in, rt_decay_logit, rt_gn_g,
           rt_w_out, final_g):
    b, l, d = x.shape
    lc = ctx.shape[1]
    lt = lc + l
    assert lc == ROW_TILE and l % ROW_TILE == 0 and ROW_TILE % GRID_W == 0
    assert d % min(WKV_LANES, d) == 0 and d % WKV_GROUP == 0 and (b * lt) % MM_TILE == 0
    assert ada_w.shape[0] == 2 and rk_w_in.shape[0] == 1 and rt_w_in.shape[0] == 1
    dk = d // RET_HEADS
    dv = rt_w_out.shape[1] // RET_HEADS

    n_mod = -(-(b + 1) // 8) * 8
    cv = jnp.concatenate([c, c_ctx[None], jnp.zeros((n_mod - b - 1, d), F32)], axis=0)
    mod = _adaln(cv, ada_w, ada_b).reshape(2, n_mod, 3, d)

    xm, lw, a = _rk_pro(x, ctx, mod[0], norm_g[0], rk_mix[0], rk_w1[0].astype(BF16),
                        rk_w2[0].astype(BF16), rk_a1[0].astype(BF16), rk_a2[0].astype(BF16),
                        rk_w0[0], rk_a0[0])
    proj = _rk_proj(xm.reshape(4, b * lt, d), rk_w_in[0].astype(BF16)).reshape(4, b, lt, d)
    k_k = rk_k_k[0].reshape(1, d)
    k_a = rk_k_a[0].reshape(1, d)
    n_ctx_blocks = lc // ROW_TILE
    o_f = _wkv(proj, lw, a, k_k, k_a, 0, n_ctx_blocks)
    o_sum = _wkv(proj, lw, a, k_k, k_a, 1, n_ctx_blocks, prev=o_f)
    x1, h1 = _rk_out(o_sum, proj, a, x, ctx, mod[0], mod[1], norm_g[1], k_a,
                     rk_r_k[0].reshape(1, d), rk_ln_g[0].reshape(1, d), rk_ln_b[0].reshape(1, d),
                     rk_w_out[0].astype(BF16))

    cos, sin = _rope_tables(l, lc, dk)
    qkvg = _rt_proj(h1.reshape(b * lt, d), rt_w_in[0].astype(BF16), jnp.tile(cos, (b, 1)),
                    jnp.tile(sin, (b, 1)), dk)
    y = _ret(qkvg.reshape(b, lt, -1), rt_decay_logit[0], rt_gn_g[0], lc // RET_CHUNK, dk, dv)
    return _rt_out(y, qkvg.reshape(b, lt, -1), x1, mod[1], final_g, rt_w_out[0].astype(BF16),
                   lc // ROW_TILE)
```

```python
import functools
import math

import jax
import jax.numpy as jnp
from jax import lax
from jax.experimental import pallas as pl
from jax.experimental.pallas import tpu as pltpu

F32 = jnp.float32
BF16 = jnp.bfloat16

GRID_W = 64
EPS = 1e-6
RWKV_HEAD = 64
RWKV_GN_EPS = 64e-5
RET_HEADS = 8
ROPE_BASE = 10000.0

LANE = 128
MXU_DIM = 256
VMEM_LIMIT = 56 * 1024 * 1024

ROW_TILE = 256
OUT_TILE = 128
RK_OUT_SPLITS = 2
MM_TILE = 512
RT_PROJ_TILE = 1024
WKV_CHUNK = 64
WKV_GROUP = MXU_DIM
WKV_STAGGER = 3
WKV_LANES = 2048
RET_CHUNK = 256
ACT_DTYPE = BF16


def _cparams(sem, vmem=VMEM_LIMIT):
    return pltpu.CompilerParams(dimension_semantics=sem, vmem_limit_bytes=vmem)


def _mm(a, b):
    return jnp.dot(a.astype(BF16), b.astype(BF16), preferred_element_type=F32)


def _mm_nt(a, b):
    return lax.dot_general(a.astype(BF16), b.astype(BF16), (((1,), (1,)), ((), ())),
                           preferred_element_type=F32)


def _mm_tn(a, b):
    return lax.dot_general(a.astype(BF16), b.astype(BF16), (((0,), (0,)), ((), ())),
                           preferred_element_type=F32)


def _split2(x):
    hi = x.astype(BF16)
    lo = (x - hi.astype(F32)).astype(BF16)
    return hi, lo


def _split3(x):
    hi = x.astype(BF16)
    r = x - hi.astype(F32)
    mid = r.astype(BF16)
    lo = (r - mid.astype(F32)).astype(BF16)
    return hi, mid, lo


def _dot(a, b, dims):
    return lax.dot_general(a, b, dims, preferred_element_type=F32)


_NN = (((1,), (0,)), ((), ()))
_NT = (((1,), (1,)), ((), ()))
_TN = (((0,), (0,)), ((), ()))


def _mm3(a, b, dims=_NN):
    a1, a2 = _split2(a)
    b1, b2 = _split2(b)
    return _dot(a1, b1, dims) + (_dot(a1, b2, dims) + _dot(a2, b1, dims))


def _mm_exact_lhs(a_bf16, b):
    b1, b2, b3 = _split3(b)
    return _dot(a_bf16, b1, _NN) + (_dot(a_bf16, b2, _NN) + _dot(a_bf16, b3, _NN))


def _mm_exact_rhs(a, b_bf16):
    a1, a2, a3 = _split3(a)
    return _dot(a1, b_bf16, _NN) + (_dot(a2, b_bf16, _NN) + _dot(a3, b_bf16, _NN))


def _sigmoid(x):
    return 0.5 * jnp.tanh(0.5 * x) + 0.5


def _silu(x):
    h = 0.5 * x
    return h + h * jnp.tanh(h)


def _norm_mod(xb, g, mod, cols=None):
    inv = lax.rsqrt(jnp.mean(xb * xb, axis=-1, keepdims=True) + EPS)
    gain = g * (1.0 + mod[1:2])
    shift = mod[0:1]
    if cols is not None:
        xb, gain, shift = xb[:, cols], gain[:, cols], shift[:, cols]
    return xb * inv * gain + shift


def _adaln_kernel(cv_ref, w_ref, b_ref, o_ref):
    s = _silu(cv_ref[...])
    o_ref[...] = _mm3(s, w_ref[...]) + b_ref[...]


def _adaln(cv, ada_w, ada_b):
    depth, d, d3 = ada_w.shape
    rows = cv.shape[0]
    tn = math.gcd(d3, 6 * LANE)
    return pl.pallas_call(
        _adaln_kernel,
        out_shape=jax.ShapeDtypeStruct((depth, rows, d3), F32),
        grid=(depth, d3 // tn),
        in_specs=[
            pl.BlockSpec((rows, d), lambda i, n: (0, 0)),
            pl.BlockSpec((None, d, tn), lambda i, n: (i, 0, n)),
            pl.BlockSpec((None, 1, tn), lambda i, n: (i, 0, n)),
        ],
        out_specs=pl.BlockSpec((None, rows, tn), lambda i, n: (i, 0, n)),
        compiler_params=_cparams(("parallel", "parallel")),
        name="adaln",
    )(cv, ada_w, ada_b.reshape(depth, 1, d3))


def _rk_pro_kernel(ctx_ref, x_ref, xp_ref, xn_ref, modc_ref, modl_ref, g_ref, mix_ref,
                   w1_ref, w2_ref, a1_ref, a2_ref, w0_ref, a0_ref, o_ref, lw_ref, a_ref,
                   *, n_lat_tiles):
    j = pl.program_id(1)
    g = g_ref[...]
    mix = mix_ref[...]
    t, d = x_ref.shape
    row = lax.broadcasted_iota(jnp.int32, (t, 1), 0)
    n_proj = o_ref.shape[0]

    def emit(h, hs):
        delta = hs - h
        for p in range(n_proj):
            o_ref[p] = (h + delta * mix[p:p + 1]).astype(BF16)
        xw = (h + delta * mix[n_proj:n_proj + 1]).astype(BF16)
        xa = (h + delta * mix[n_proj + 1:n_proj + 2]).astype(BF16)
        for n in range(2):
            hw = jnp.tanh(jnp.dot(xw, w1_ref[n], preferred_element_type=F32))
            ww_half = w0_ref[n:n + 1, :] + _mm(hw, w2_ref[n])
            half_c = -0.5 * math.exp(-0.5)
            lw_ref[n] = half_c * jnp.tanh(ww_half) + half_c
            ha = jnp.dot(xa, a1_ref[n], preferred_element_type=F32)
            aa_half = a0_ref[n:n + 1, :] + _mm(ha, a2_ref[n])
            a_ref[n] = (0.5 * jnp.tanh(aa_half) + 0.5).astype(a_ref.dtype)

    @pl.when(j == 0)
    def _():
        h = _norm_mod(ctx_ref[...], g, modc_ref[...])
        half = d // 2
        prev = jnp.where(row == 0, 0.0, pltpu.roll(h[:, :half], 1, 0))
        nxt = jnp.where(row == t - 1, 0.0, pltpu.roll(h[:, half:], t - 1, 0))
        emit(h, jnp.concatenate([prev, nxt], axis=-1))

    @pl.when(j > 0)
    def _():
        mod = modl_ref[...]
        q = d // 4
        h = _norm_mod(x_ref[...], g, mod)
        hp = _norm_mod(xp_ref[...], g, mod, cols=slice(2 * q, 3 * q))
        hn = _norm_mod(xn_ref[...], g, mod, cols=slice(3 * q, 4 * q))
        col = row % GRID_W
        left = jnp.where(col == 0, 0.0, pltpu.roll(h[:, :q], 1, 0))
        right = jnp.where(col == GRID_W - 1, 0.0, pltpu.roll(h[:, q:2 * q], t - 1, 0))
        up = jnp.concatenate([hp, h[:t - GRID_W, 2 * q:3 * q]], axis=0)
        up = jnp.where((j == 1) & (row < GRID_W), 0.0, up)
        down = jnp.concatenate([h[GRID_W:, 3 * q:], hn], axis=0)
        down = jnp.where((j == n_lat_tiles) & (row >= t - GRID_W), 0.0, down)
        emit(h, jnp.concatenate([left, right, up, down], axis=-1))


def _rk_pro(x, ctx, mod0, norm_g0, mix, w1, w2, a1, a2, w0, a0):
    b, l, d = x.shape
    t = ROW_TILE
    n_lat = l // t
    sub = t // GRID_W
    n_rows = l // GRID_W
    xr = x.reshape(b, n_rows, GRID_W, d)
    nmix = mix.shape[0]
    n_proj = nmix - 2
    r = w1.shape[-1]
    lt = ctx.shape[1] + l
    full = lambda shape: pl.BlockSpec(shape, lambda bi, j: (0,) * len(shape))
    tok2 = pl.BlockSpec((2, None, t, d), lambda bi, j: (0, bi, j, 0))
    return pl.pallas_call(
        functools.partial(_rk_pro_kernel, n_lat_tiles=n_lat),
        out_shape=(jax.ShapeDtypeStruct((n_proj, b, lt, d), BF16),
                   jax.ShapeDtypeStruct((2, b, lt, d), F32),
                   jax.ShapeDtypeStruct((2, b, lt, d), ACT_DTYPE)),
        grid=(b, n_lat + 1),
        in_specs=[
            pl.BlockSpec((None, t, d), lambda bi, j: (bi, 0, 0)),
            pl.BlockSpec((None, t, d), lambda bi, j: (bi, jnp.maximum(j - 1, 0), 0)),
            pl.BlockSpec((None, None, GRID_W, d),
                         lambda bi, j: (bi, jnp.maximum((j - 1) * sub - 1, 0), 0, 0)),
            pl.BlockSpec((None, None, GRID_W, d),
                         lambda bi, j: (bi, jnp.minimum(jnp.maximum(j, 1) * sub, n_rows - 1), 0, 0)),
            pl.BlockSpec((None, 3, d), lambda bi, j: (b, 0, 0)),
            pl.BlockSpec((None, 3, d), lambda bi, j: (bi, 0, 0)),
            pl.BlockSpec((1, d), lambda bi, j: (0, 0)),
            pl.BlockSpec((nmix, d), lambda bi, j: (0, 0)),
            full((2, d, r)), full((2, r, d)), full((2, d, r)), full((2, r, d)),
            full((2, d)), full((2, d)),
        ],
        out_specs=(pl.BlockSpec((n_proj, None, t, d), lambda bi, j: (0, bi, j, 0)), tok2, tok2),
        compiler_params=_cparams(("parallel", "parallel")),
        name="rk_pro",
    )(ctx, x, xr, xr, mod0, mod0, norm_g0.reshape(1, d), mix, w1, w2, a1, a2, w0, a0)


def _rk_proj_kernel(x_ref, w_ref, o_ref):
    o_ref[...] = jnp.dot(x_ref[...], w_ref[...], preferred_element_type=F32).astype(o_ref.dtype)


def _rk_proj(xm, w_in):
    nproj, d, _ = w_in.shape
    m = xm.shape[1]
    tm = RT_PROJ_TILE if m % RT_PROJ_TILE == 0 else MM_TILE
    return pl.pallas_call(
        _rk_proj_kernel,
        out_shape=jax.ShapeDtypeStruct((nproj, m, d), ACT_DTYPE),
        grid=(nproj, m // tm),
        in_specs=[
            pl.BlockSpec((None, tm, d), lambda p, i: (p, i, 0)),
            pl.BlockSpec((None, d, d), lambda p, i: (p, 0, 0)),
        ],
        out_specs=pl.BlockSpec((None, tm, d), lambda p, i: (p, i, 0)),
        compiler_params=_cparams(("parallel", "parallel")),
        name="rk_proj",
    )(xm, w_in)


def _wkv_kernel(r_ref, k_ref, v_ref, lw_ref, a_ref, kk_ref, ka_ref, *rest, reverse, chunks_per_block):
    prev_ref = rest[0] if len(rest) == 3 else None
    o_ref, t_ref = rest[-2:]
    c = WKV_CHUNK
    gw = WKV_GROUP
    nq = r_ref.shape[-1] // gw
    nh = gw // RWKV_HEAD
    hb = gw
    s = pl.program_id(2)

    @pl.when(s == 0)
    def _():
        t_ref[...] = jnp.zeros_like(t_ref)

    row_id = lax.broadcasted_iota(jnp.int32, (c, 1), 0)

    def running_sum(x):
        sh = 1
        while sh < c:
            if reverse:
                x = x + jnp.where(row_id < c - sh, pltpu.roll(x, c - sh, 0), 0.0)
            else:
                x = x + jnp.where(row_id >= sh, pltpu.roll(x, sh, 0), 0.0)
            sh *= 2
        return x

    rw = lax.broadcasted_iota(jnp.int32, (c, nh * c), 0)
    cw = lax.broadcasted_iota(jnp.int32, (c, nh * c), 1) % c
    strict_w = (cw > rw) if reverse else (cw < rw)
    incl_w = strict_w | (cw == rw)
    eye_w = (cw == rw).astype(F32)
    rs = lax.broadcasted_iota(jnp.int32, (nh * c, nh * c), 0)
    cs = lax.broadcasted_iota(jnp.int32, (nh * c, nh * c), 1)
    bd_mask = ((rs // c) == (cs // c)).astype(BF16)
    rl = lax.broadcasted_iota(jnp.int32, (hb, hb), 0)
    cl = lax.broadcasted_iota(jnp.int32, (hb, hb), 1)
    head_bd = (rl // RWKV_HEAD) == (cl // RWKV_HEAD)
    ones_bd = head_bd.astype(BF16)
    lane_head = lax.broadcasted_iota(jnp.int32, (1, hb), 1) // RWKV_HEAD
    lane_masks = [(lane_head == h).astype(BF16) for h in range(nh)]
    end_row = 0 if reverse else c - 1

    def stack(x):
        xb = x.astype(BF16)
        return jnp.concatenate([xb * m for m in lane_masks], axis=0)

    def block_diag(xw):
        return jnp.concatenate([xw.astype(BF16)] * nh, axis=0) * bd_mask

    def prepass(cidx, q, out):
        rows = pl.ds(cidx * c, c)
        lanes = pl.ds(q * gw, gw)
        r = r_ref[rows, lanes].astype(F32)
        k = k_ref[rows, lanes].astype(F32)
        v = v_ref[rows, lanes].astype(F32)
        lw = lw_ref[rows, lanes]
        a = a_ref[rows, lanes].astype(F32)
        kap = ka_ref[:, lanes]

        b = running_sum(lw)
        kk = k * kk_ref[:, lanes]
        ssq = ssq_blocks[q][cidx * c:(cidx + 1) * c]
        b_end = b[end_row:end_row + 1, :]
        e_pos = jnp.exp(b)
        e_neg = jnp.exp(-b)
        e_prev = jnp.exp(b - lw)
        e_end = jnp.exp(b_end - b)
        p_end = jnp.exp(b_end)

        kappa = kk / jnp.maximum(jnp.sqrt(ssq), 1e-12)
        kd = k * (1.0 + (a - 1.0) * kap)
        beta = kappa * a

        kt = kappa * e_prev
        rt = r * e_pos
        bt_s = stack(beta * e_neg)
        kn_s = stack(kd * e_neg)
        v_s = stack(v)
        bbar = beta * e_end
        kbar = kd * e_end

        lhs = jnp.concatenate([kt, rt], axis=0)
        ab = _mm_nt(lhs, bt_s)
        ak = _mm_nt(lhs, kn_s)
        yield
        a_ab = jnp.where(strict_w, ab[:c], 0.0)
        a_rb = jnp.where(incl_w, ab[c:], 0.0)
        a_ak = jnp.where(strict_w, ak[:c], 0.0)
        a_rk = jnp.where(incl_w, ak[c:], 0.0)

        pw = -a_ab
        minv = eye_w + pw
        pw = _mm(pw, block_diag(pw))
        intra_v = _mm(jnp.concatenate([a_ak, a_rk], axis=0), v_s)
        av = intra_v[:c]
        ark_v = intra_v[c:]
        yield
        n_sq = int(math.log2(c)) - 1
        for i in range(n_sq):
            both = _mm(jnp.concatenate([minv, pw], axis=0) if i < n_sq - 1 else minv, block_diag(pw))
            minv = minv + both[:c]
            pw = both[c:]
            yield

        keys = jnp.concatenate([kbar, bbar], axis=0).astype(BF16)
        out[(cidx, q)] = (lhs.astype(BF16), minv.astype(BF16), a_rb.astype(BF16), av, ark_v,
                          v.astype(BF16), keys, p_end)

    order = list(range(chunks_per_block))
    if reverse:
        order.reverse()
    factors = {}
    ssq_blocks = []
    for q in range(nq):
        kk_blk = k_ref[:, pl.ds(q * gw, gw)].astype(F32) * kk_ref[:, pl.ds(q * gw, gw)]
        ssq_blocks.append(_mm(kk_blk * kk_blk, ones_bd))
    def recurrence():
        for cidx in order:
            while any((cidx, q) not in factors for q in range(nq)):
                yield
            rows = pl.ds(cidx * c, c)
            fac = [factors[(cidx, q)] for q in range(nq)]
            reads = [_dot(f[0], t_ref[q].astype(BF16), _NT) for q, f in enumerate(fac)]
            yield
            us = [-_dot(f[1], stack(reads[q][:c] + f[3]), _NN) for q, f in enumerate(fac)]
            yield
            for q, f in enumerate(fac):
                _, _, a_rb, _, ark_v, vb, keys, p_end = f
                o = reads[q][c:] + ark_v + _dot(a_rb, stack(us[q]), _NN)
                if prev_ref is not None:
                    o = o + prev_ref[rows, pl.ds(q * gw, gw)].astype(F32)
                o_ref[rows, pl.ds(q * gw, gw)] = o.astype(o_ref.dtype)
                vu = jnp.concatenate([vb, us[q].astype(BF16)], axis=0)
                t_ref[q] = t_ref[q] * p_end + jnp.where(head_bd, _dot(vu, keys, _TN), 0.0)
            yield

    chains = [prepass(cidx, q, factors) for cidx in order for q in range(nq)]
    start = [(i // nq) * WKV_STAGGER for i in range(len(chains))]
    live = list(range(len(chains)))
    scan = recurrence()
    rnd = 0
    while scan is not None:
        live = [i for i in live if start[i] > rnd or next(chains[i], "done") != "done"]
        if next(scan, "done") == "done":
            scan = None
        rnd += 1


def _wkv(proj, lw, a, k_k, k_a, direction, n_ctx_blocks, prev=None):
    _, b, lt, d = proj.shape
    rb = ROW_TILE
    hb = min(WKV_LANES, d)
    nblk = lt // rb
    reverse = direction == 1

    if reverse:
        def blk(s):
            return jnp.where(s < n_ctx_blocks, n_ctx_blocks - 1 - s, nblk - 1 - (s - n_ctx_blocks))
    else:
        def blk(s):
            return s

    def tok(p):
        return pl.BlockSpec((None, None, rb, hb), lambda bi, hi, s: (p, bi, blk(s), hi))

    out_spec = pl.BlockSpec((None, rb, hb), lambda bi, hi, s: (bi, blk(s), hi))
    extra = () if prev is None else (prev,)
    return pl.pallas_call(
        functools.partial(_wkv_kernel, reverse=reverse, chunks_per_block=rb // WKV_CHUNK),
        out_shape=jax.ShapeDtypeStruct((b, lt, d), ACT_DTYPE),
        grid=(b, d // hb, nblk),
        in_specs=[tok(0), tok(1), tok(2), tok(direction), tok(direction),
                  pl.BlockSpec((1, hb), lambda bi, hi, s: (0, hi)),
                  pl.BlockSpec((1, hb), lambda bi, hi, s: (0, hi))] + [out_spec] * len(extra),
        out_specs=out_spec,
        scratch_shapes=[pltpu.VMEM((hb // WKV_GROUP, WKV_GROUP, WKV_GROUP), F32)],
        compiler_params=_cparams(("parallel", "parallel", "arbitrary")),
        name="wkv_bwd" if reverse else "wkv_fwd",
    )(proj, proj, proj, lw, a, k_k, k_a, *extra)


def _head_sums(x, ones_bd):
    w = ones_bd.shape[0]
    return jnp.concatenate(
        [_mm(x[:, i:i + w], ones_bd) for i in range(0, x.shape[-1], w)], axis=-1)


def _rk_out_kernel(o_ref, r_ref, k_ref, v_ref, g_ref, a_ref, x_ref, ctx_ref,
                   modl_ref, modc_ref, modl1_ref, modc1_ref, ng1_ref,
                   ka_ref, rk_ref, lng_ref, lnb_ref, w_ref,
                   x1_ref, h1_ref, *, n_ctx_tiles):
    is_ctx = pl.program_id(1) < n_ctx_tiles
    gate = jnp.where(is_ctx, modc_ref[2:3, :], modl_ref[2:3, :])
    mod1 = jnp.where(is_ctx, modc1_ref[...], modl1_ref[...])
    ng1 = ng1_ref[...]
    hw = min(MXU_DIM, r_ref.shape[-1])
    rl = lax.broadcasted_iota(jnp.int32, (hw, hw), 0)
    cl = lax.broadcasted_iota(jnp.int32, (hw, hw), 1)
    same_head = (rl // RWKV_HEAD) == (cl // RWKV_HEAD)
    ones_bd = same_head.astype(BF16)
    mean_bd = jnp.where(same_head, 1.0 / RWKV_HEAD, 0.0).astype(BF16)

    groups = [pl.ds(i, hw) for i in range(0, r_ref.shape[-1], hw)]
    f32 = lambda ref, *idx: ref[idx].astype(F32)
    sub = min(OUT_TILE, o_ref.shape[0])
    for r0 in range(0, o_ref.shape[0], sub):
        rows = pl.ds(r0, sub)
        mu = [_mm(o_ref[rows, gs], mean_bd) for gs in groups]
        rkk = []
        for gs in groups:
            ka = ka_ref[:, gs]
            rk = rk_ref[:, gs]
            a_sum = (a_ref[0, rows, gs] + a_ref[1, rows, gs]).astype(F32)
            scale = (2.0 - 2.0 * ka) * rk + a_sum * (ka * rk)
            rkk.append(_mm((r_ref[rows, gs] * k_ref[rows, gs]).astype(F32) * scale, ones_bd))
        oc = [f32(o_ref, rows, gs) - mg for gs, mg in zip(groups, mu)]
        var = [_mm(cg * cg, mean_bd) for cg in oc]
        out = None
        per = max(1, len(groups) // RK_OUT_SPLITS)
        for g0 in range(0, len(groups), per):
            ys = []
            for gi in range(g0, min(g0 + per, len(groups))):
                gs = groups[gi]
                on = oc[gi] * lax.rsqrt(var[gi] + RWKV_GN_EPS) * lng_ref[:, gs] + lnb_ref[:, gs]
                y = (on + rkk[gi] * f32(v_ref, rows, gs)) * _silu(f32(g_ref, rows, gs))
                ys.append(y.astype(BF16))
            part = _dot(jnp.concatenate(ys, axis=-1), w_ref[pl.ds(g0 * hw, len(ys) * hw), :], _NN)
            out = part if out is None else out + part
        new = jnp.where(is_ctx, ctx_ref[rows, :], x_ref[rows, :]) + gate * out
        x1_ref[rows, :] = new
        h1_ref[rows, :] = _norm_mod(new, ng1, mod1).astype(BF16)


def _rk_out(o_sum, proj, a, x, ctx, mod0, mod1, norm_g1, k_a, r_k, ln_g, ln_b, w_out):
    b, l, d = x.shape
    lc = ctx.shape[1]
    lt = lc + l
    t = ROW_TILE
    nct = lc // t
    tok = pl.BlockSpec((None, t, d), lambda bi, j: (bi, j, 0))
    ptok = lambda p: pl.BlockSpec((None, None, t, d), lambda bi, j: (p, bi, j, 0))
    vec = pl.BlockSpec((1, d), lambda bi, j: (0, 0))
    lat_blk = lambda bi, j: (bi, jnp.maximum(j - nct, 0), 0)
    mod_l = pl.BlockSpec((None, 3, d), lambda bi, j: (bi, 0, 0))
    mod_c = pl.BlockSpec((None, 3, d), lambda bi, j: (b, 0, 0))
    return pl.pallas_call(
        functools.partial(_rk_out_kernel, n_ctx_tiles=nct),
        out_shape=(jax.ShapeDtypeStruct((b, l, d), F32),
                   jax.ShapeDtypeStruct((b, lt, d), BF16)),
        grid=(b, lt // t),
        in_specs=[tok, ptok(0), ptok(1), ptok(2), ptok(3),
                  pl.BlockSpec((2, None, t, d), lambda bi, j: (0, bi, j, 0)),
                  pl.BlockSpec((None, t, d), lat_blk),
                  pl.BlockSpec((None, lc, d), lambda bi, j: (bi, 0, 0)),
                  mod_l, mod_c, mod_l, mod_c, vec, vec, vec, vec, vec,
                  pl.BlockSpec((d, d), lambda bi, j: (0, 0))],
        out_specs=(pl.BlockSpec((None, t, d), lat_blk),
                   pl.BlockSpec((None, t, d), lambda bi, j: (bi, j, 0))),
        compiler_params=_cparams(("parallel", "arbitrary")),
        name="rk_out",
    )(o_sum, proj, proj, proj, proj, a, x, ctx, mod0, mod0, mod1, mod1,
      norm_g1.reshape(1, d), k_a, r_k, ln_g, ln_b, w_out)


def _rt_proj_kernel(h_ref, w_ref, cos_ref, sin_ref, o_ref, *, dk):
    n = pl.program_id(0)
    tn = w_ref.shape[-1]
    half = dk // 2
    sub = min(ROW_TILE, h_ref.shape[0])
    chunks = [pl.ds(i, sub) for i in range(0, h_ref.shape[0], sub)]

    def project(rows):
        return jnp.dot(h_ref[rows, :], w_ref[...], preferred_element_type=F32)

    def rotate(scale):
        for rows in chunks:
            acc = project(rows)
            cos = cos_ref[rows, :]
            sin = sin_ref[rows, :]
            parts = []
            for h in range(tn // dk):
                x1 = acc[:, h * dk:h * dk + half]
                x2 = acc[:, h * dk + half:(h + 1) * dk]
                parts.append((x1 * cos - x2 * sin) * scale)
                parts.append((x1 * sin + x2 * cos) * scale)
            o_ref[rows, :] = jnp.concatenate(parts, axis=-1).astype(BF16)

    @pl.when(n == 0)
    def _():
        rotate(1.0)

    @pl.when(n == 1)
    def _():
        rotate(dk ** -0.5)

    @pl.when(n > 1)
    def _():
        for rows in chunks:
            o_ref[rows, :] = project(rows).astype(BF16)


def _rt_proj(h1, w_in, cos, sin, dk):
    m, d = h1.shape
    n_out = w_in.shape[1]
    tn = d
    tm = RT_PROJ_TILE if m % RT_PROJ_TILE == 0 else MM_TILE
    half = dk // 2
    return pl.pallas_call(
        functools.partial(_rt_proj_kernel, dk=dk),
        out_shape=jax.ShapeDtypeStruct((m, n_out), BF16),
        grid=(n_out // tn, m // tm),
        in_specs=[
            pl.BlockSpec((tm, d), lambda n, i: (i, 0)),
            pl.BlockSpec((d, tn), lambda n, i: (0, n)),
            pl.BlockSpec((tm, half), lambda n, i: (i, 0)),
            pl.BlockSpec((tm, half), lambda n, i: (i, 0)),
        ],
        out_specs=pl.BlockSpec((tm, tn), lambda n, i: (i, n)),
        compiler_params=_cparams(("parallel", "parallel")),
        name="rt_proj",
    )(h1, w_in, cos, sin)


def _log_sigmoid(x):
    return jnp.minimum(x, 0.0) - jnp.log1p(jnp.exp(-jnp.abs(x)))


def _ret_bstate_kernel(lg_ref, k_ref, v_ref, rb_ref, r_ref, *, dk, dv):
    c = k_ref.shape[0]
    s = pl.program_id(1)

    @pl.when(s == 0)
    def _():
        r_ref[...] = jnp.zeros_like(r_ref)

    pos = lax.broadcasted_iota(jnp.int32, (c, 1), 0).astype(F32)
    for h in range(RET_HEADS):
        lgb = _log_sigmoid(lg_ref[1, h])[:, 0:1]
        k = k_ref[:, h * dk:(h + 1) * dk]
        v = v_ref[:, h * dv:(h + 1) * dv]
        r0 = r_ref[h]
        rb_ref[h] = r0.astype(BF16)
        kd = (k.astype(F32) * jnp.exp(pos * lgb)).astype(BF16)
        r_ref[h] = r0 * jnp.exp(float(c) * lgb) + _dot(kd, v, _TN)


def _ret_main_kernel(lg_ref, q_ref, k_ref, v_ref, rb_ref, gng_ref, y_ref, rf_ref, dm_ref, dvec_ref,
                     *, dk, dv, n_ctx_blocks):
    c = q_ref.shape[0]
    s = pl.program_id(1)

    @pl.when(s == 0)
    def _():
        rf_ref[...] = jnp.zeros_like(rf_ref)
        diff = (lax.broadcasted_iota(jnp.int32, (c, c), 0)
                - lax.broadcasted_iota(jnp.int32, (c, c), 1)).astype(F32)
        pos = lax.broadcasted_iota(jnp.int32, (c, LANE), 0).astype(F32)
        for h in range(RET_HEADS):
            lgf = _log_sigmoid(lg_ref[0, h])
            lgb = _log_sigmoid(lg_ref[1, h])
            dm_ref[h] = jnp.where(diff >= 0, jnp.exp(jnp.maximum(diff, 0.0) * lgf[:, 0:1]),
                                  jnp.exp(jnp.maximum(-diff, 0.0) * lgb[:, 0:1]))
            dvec_ref[h, 0] = jnp.exp((pos + 1.0) * lgf)
            dvec_ref[h, 1] = jnp.exp((float(c) - pos) * lgb)
            dvec_ref[h, 2] = jnp.exp((float(c) - 1.0 - pos) * lgf)

    def lanes(x, width):
        return x[:, :width] if width <= LANE else jnp.concatenate([x] * (width // LANE), axis=1)

    @pl.when(s >= n_ctx_blocks)
    def _():
        scores = []
        for h in range(RET_HEADS):
            q = q_ref[:, h * dk:(h + 1) * dk]
            k = k_ref[:, h * dk:(h + 1) * dk]
            scores.append(_dot(q, k, _NT))
        outs = []
        for h in range(RET_HEADS):
            q = q_ref[:, h * dk:(h + 1) * dk]
            sc = (scores[h] * dm_ref[h]).astype(BF16)
            lhs = jnp.concatenate([sc, q * lanes(dvec_ref[h, 0], dk).astype(BF16),
                                   q * lanes(dvec_ref[h, 1], dk).astype(BF16)], axis=1)
            rhs = jnp.concatenate([v_ref[:, h * dv:(h + 1) * dv], rf_ref[h].astype(BF16), rb_ref[h]],
                                  axis=0)
            outs.append(_dot(lhs, rhs, _NN))
        for h in range(RET_HEADS):
            o = outs[h]
            on = o * lax.rsqrt(jnp.mean(o * o, axis=-1, keepdims=True) + EPS)
            y_ref[:, h * dv:(h + 1) * dv] = (on * gng_ref[:, h * dv:(h + 1) * dv]).astype(BF16)

    for h in range(RET_HEADS):
        k = k_ref[:, h * dk:(h + 1) * dk]
        v = v_ref[:, h * dv:(h + 1) * dv]
        kd = (k.astype(F32) * lanes(dvec_ref[h, 2], dk)).astype(BF16)
        gamma_c = jnp.exp(float(c) * _log_sigmoid(lg_ref[0, h]))
        rf_ref[h] = rf_ref[h] * lanes(gamma_c, dv) + _dot(kd, v, _TN)


def _ret(qkvg, decay_logit, gn_g, n_ctx_blocks, dk, dv):
    b, lt, _ = qkvg.shape
    c = RET_CHUNK
    nblk = lt // c
    nh = RET_HEADS
    d = nh * dk
    vd = nh * dv
    assert (2 * d) % vd == 0
    voff = (2 * d) // vd
    lg = jnp.broadcast_to(decay_logit.astype(F32)[:, :, None, None], (2, nh, 1, LANE))
    lg_spec = pl.BlockSpec((2, nh, 1, LANE), lambda bi, s: (0, 0, 0, 0))

    def bwd_blk(s):
        return jnp.where(s < n_ctx_blocks, n_ctx_blocks - 1 - s, nblk - 1 - (s - n_ctx_blocks))

    r_b = pl.pallas_call(
        functools.partial(_ret_bstate_kernel, dk=dk, dv=dv),
        out_shape=jax.ShapeDtypeStruct((b, nblk, nh, dk, dv), BF16),
        grid=(b, nblk),
        in_specs=[
            lg_spec,
            pl.BlockSpec((None, c, d), lambda bi, s: (bi, bwd_blk(s), 1)),
            pl.BlockSpec((None, c, vd), lambda bi, s: (bi, bwd_blk(s), voff)),
        ],
        out_specs=pl.BlockSpec((None, None, nh, dk, dv), lambda bi, s: (bi, bwd_blk(s), 0, 0, 0)),
        scratch_shapes=[pltpu.VMEM((nh, dk, dv), F32)],
        compiler_params=_cparams(("parallel", "arbitrary")),
        name="ret_bstate",
    )(lg, qkvg, qkvg)

    lat = lambda bi, s: (bi, jnp.maximum(s - n_ctx_blocks, 0), 0)
    return pl.pallas_call(
        functools.partial(_ret_main_kernel, dk=dk, dv=dv, n_ctx_blocks=n_ctx_blocks),
        out_shape=jax.ShapeDtypeStruct((b, lt - n_ctx_blocks * c, vd), BF16),
        grid=(b, nblk),
        in_specs=[
            lg_spec,
            pl.BlockSpec((None, c, d), lambda bi, s: (bi, s, 0)),
            pl.BlockSpec((None, c, d), lambda bi, s: (bi, s, 1)),
            pl.BlockSpec((None, c, vd), lambda bi, s: (bi, s, voff)),
            pl.BlockSpec((None, None, nh, dk, dv), lambda bi, s: (bi, s, 0, 0, 0)),
            pl.BlockSpec((1, vd), lambda bi, s: (0, 0)),
        ],
        out_specs=pl.BlockSpec((None, c, vd), lat),
        scratch_shapes=[pltpu.VMEM((nh, dk, dv), F32), pltpu.VMEM((nh, c, c), F32),
                        pltpu.VMEM((nh, 3, c, LANE), F32)],
        compiler_params=_cparams(("parallel", "arbitrary")),
        name="ret_main",
    )(lg, qkvg, qkvg, qkvg, r_b, gn_g.reshape(1, vd))


def _rt_out_kernel(y_ref, g_ref, x_ref, mod_ref, fg_ref, w_ref, o_ref):
    gated = y_ref[...].astype(F32) * _silu(g_ref[...].astype(F32))
    new = x_ref[...] + mod_ref[2:3, :] * _mm(gated, w_ref[...])
    o_ref[...] = new * lax.rsqrt(jnp.mean(new * new, axis=-1, keepdims=True) + EPS) * fg_ref[...]


def _rt_out(y, qkvg, x1, mod1, final_g, w_out, n_ctx_tiles):
    b, l, d = x1.shape
    vd = w_out.shape[0]
    t = ROW_TILE
    goff = (qkvg.shape[-1] - vd) // vd
    return pl.pallas_call(
        _rt_out_kernel,
        out_shape=jax.ShapeDtypeStruct((b, l, d), F32),
        grid=(b, l // t),
        in_specs=[
            pl.BlockSpec((None, t, vd), lambda bi, j: (bi, j, 0)),
            pl.BlockSpec((None, t, vd), lambda bi, j: (bi, j + n_ctx_tiles, goff)),
            pl.BlockSpec((None, t, d), lambda bi, j: (bi, j, 0)),
            pl.BlockSpec((None, 3, d), lambda bi, j: (bi, 0, 0)),
            pl.BlockSpec((1, d), lambda bi, j: (0, 0)),
            pl.BlockSpec((vd, d), lambda bi, j: (0, 0)),
        ],
        out_specs=pl.BlockSpec((None, t, d), lambda bi, j: (bi, j, 0)),
        compiler_params=_cparams(("parallel", "parallel")),
        name="rt_out",
    )(y, qkvg, x1, mod1, final_g.reshape(1, d), w_out)


def _rope_tables(l, lc, dk):
    t = jnp.arange(l)
    row = (t // GRID_W).astype(F32)
    col = (t % GRID_W).astype(F32)
    nf = dk // 4
    inv = ROPE_BASE ** (-jnp.arange(nf, dtype=F32) / nf)
    ang = jnp.concatenate([row[:, None] * inv, col[:, None] * inv], axis=-1)
    cos = jnp.concatenate([jnp.ones((lc, dk // 2), F32), jnp.cos(ang)], axis=0)
    sin = jnp.concatenate([jnp.zeros((lc, dk // 2), F32), jnp.sin(ang)], axis=0)
    return cos, sin


def kernel(x, c, ctx, c_ctx, ada_w, ada_b, norm_g, rk_mix, rk_w_in, rk_w0, rk_w1, rk_w2, rk_a0, rk_a1,
           rk_a2, rk_k_k, rk_k_a, rk_r_k, rk_ln_g, rk_ln_b, rk_w_out, rt_w_in, rt_decay_logit, rt_gn_g,
           rt_w_out, final_g):
    b, l, d = x.shape
    lc = ctx.shape[1]
    lt = lc + l
    assert lc == ROW_TILE and l % ROW_TILE == 0 and ROW_TILE % GRID_W == 0
    assert d % min(WKV_LANES, d) == 0 and d % WKV_GROUP == 0 and (b * lt) % MM_TILE == 0
    assert ada_w.shape[0] == 2 and rk_w_in.shape[0] == 1 and rt_w_in.shape[0] == 1
    dk = d // RET_HEADS
    dv = rt_w_out.shape[1] // RET_HEADS

    n_mod = -(-(b + 1) // 8) * 8
    cv = jnp.concatenate([c, c_ctx[None], jnp.zeros((n_mod - b - 1, d), F32)], axis=0)
    mod = _adaln(cv, ada_w, ada_b).reshape(2, n_mod, 3, d)

    xm, lw, a = _rk_pro(x, ctx, mod[0], norm_g[0], rk_mix[0], rk_w1[0].astype(BF16),
                        (0.5 * rk_w2[0]).astype(BF16), rk_a1[0].astype(BF16),
                        (0.5 * rk_a2[0]).astype(BF16), 0.5 * rk_w0[0], 0.5 * rk_a0[0])
    proj = _rk_proj(xm.reshape(4, b * lt, d), rk_w_in[0].astype(BF16)).reshape(4, b, lt, d)
    k_k = rk_k_k[0].reshape(1, d)
    k_a = rk_k_a[0].reshape(1, d)
    n_ctx_blocks = lc // ROW_TILE
    o_f = _wkv(proj, lw, a, k_k, k_a, 0, n_ctx_blocks)
    o_sum = _wkv(proj, lw, a, k_k, k_a, 1, n_ctx_blocks, prev=o_f)
    x1, h1 = _rk_out(o_sum, proj, a, x, ctx, mod[0], mod[1], norm_g[1], k_a,
                     rk_r_k[0].reshape(1, d), rk_ln_g[0].reshape(1, d), rk_ln_b[0].reshape(1, d),
                     rk_w_out[0].astype(BF16))

    cos, sin = _rope_tables(l, lc, dk)
    qkvg = _rt_proj(h1.reshape(b * lt, d), rt_w_in[0].astype(BF16), jnp.tile(cos, (b, 1)),
                    jnp.tile(sin, (b, 1)), dk)
    y = _ret(qkvg.reshape(b, lt, -1), rt_decay_logit[0], rt_gn_g[0], lc // RET_CHUNK, dk, dv)
    return _rt_out(y, qkvg.reshape(b, lt, -1), x1, mod[1], final_g, rt_w_out[0].astype(BF16),
                   lc // ROW_TILE)
```

```python
import functools
import math

import jax
import jax.numpy as jnp
from jax import lax
from jax.experimental import pallas as pl
from jax.experimental.pallas import tpu as pltpu

F32 = jnp.float32
BF16 = jnp.bfloat16

GRID_W = 64
EPS = 1e-6
RWKV_HEAD = 64
RWKV_GN_EPS = 64e-5
RET_HEADS = 8
ROPE_BASE = 10000.0

LANE = 128
MXU_DIM = 256
VMEM_LIMIT = 56 * 1024 * 1024

ROW_TILE = 256
OUT_TILE = 128
RK_OUT_SPLITS = 2
MM_TILE = 512
RT_PROJ_TILE = 1024
WKV_CHUNK = 64
WKV_GROUP = MXU_DIM
WKV_STAGGER = 3
WKV_LANES = 2048
RET_CHUNK = 256
ACT_DTYPE = BF16


def _cparams(sem, vmem=VMEM_LIMIT):
    return pltpu.CompilerParams(dimension_semantics=sem, vmem_limit_bytes=vmem)


def _mm(a, b):
    return jnp.dot(a.astype(BF16), b.astype(BF16), preferred_element_type=F32)


def _mm_nt(a, b):
    return lax.dot_general(a.astype(BF16), b.astype(BF16), (((1,), (1,)), ((), ())),
                           preferred_element_type=F32)


def _split2(x):
    hi = x.astype(BF16)
    lo = (x - hi.astype(F32)).astype(BF16)
    return hi, lo


def _dot(a, b, dims):
    return lax.dot_general(a, b, dims, preferred_element_type=F32)


_NN = (((1,), (0,)), ((), ()))
_NT = (((1,), (1,)), ((), ()))
_TN = (((0,), (0,)), ((), ()))


def _mm3(a, b, dims=_NN):
    a1, a2 = _split2(a)
    b1, b2 = _split2(b)
    return _dot(a1, b1, dims) + (_dot(a1, b2, dims) + _dot(a2, b1, dims))


def _silu(x):
    h = 0.5 * x
    return h + h * jnp.tanh(h)


def _norm_mod(xb, g, mod, cols=None):
    inv = lax.rsqrt(jnp.mean(xb * xb, axis=-1, keepdims=True) + EPS)
    gain = g * (1.0 + mod[1:2])
    shift = mod[0:1]
    if cols is not None:
        xb, gain, shift = xb[:, cols], gain[:, cols], shift[:, cols]
    return xb * inv * gain + shift


def _adaln_kernel(cv_ref, w_ref, b_ref, o_ref):
    s = _silu(cv_ref[...])
    o_ref[...] = _mm3(s, w_ref[...]) + b_ref[...]


def _adaln(cv, ada_w, ada_b):
    depth, d, d3 = ada_w.shape
    rows = cv.shape[0]
    tn = math.gcd(d3, 6 * LANE)
    return pl.pallas_call(
        _adaln_kernel,
        out_shape=jax.ShapeDtypeStruct((depth, rows, d3), F32),
        grid=(depth, d3 // tn),
        in_specs=[
            pl.BlockSpec((rows, d), lambda i, n: (0, 0)),
            pl.BlockSpec((None, d, tn), lambda i, n: (i, 0, n)),
            pl.BlockSpec((None, 1, tn), lambda i, n: (i, 0, n)),
        ],
        out_specs=pl.BlockSpec((None, rows, tn), lambda i, n: (i, 0, n)),
        compiler_params=_cparams(("parallel", "parallel")),
        name="adaln",
    )(cv, ada_w, ada_b.reshape(depth, 1, d3))


def _rk_pro_kernel(ctx_ref, x_ref, xp_ref, xn_ref, modc_ref, modl_ref, g_ref, mix_ref,
                   w1_ref, w2_ref, a1_ref, a2_ref, w0_ref, a0_ref, o_ref, lw_ref, a_ref,
                   *, n_lat_tiles):
    j = pl.program_id(1)
    g = g_ref[...]
    mix = mix_ref[...]
    t, d = x_ref.shape
    row = lax.broadcasted_iota(jnp.int32, (t, 1), 0)
    n_proj = o_ref.shape[0]

    def emit(h, hs):
        delta = hs - h
        for p in range(n_proj):
            o_ref[p] = (h + delta * mix[p:p + 1]).astype(BF16)
        xw = (h + delta * mix[n_proj:n_proj + 1]).astype(BF16)
        xa = (h + delta * mix[n_proj + 1:n_proj + 2]).astype(BF16)
        for n in range(2):
            hw = jnp.tanh(jnp.dot(xw, w1_ref[n], preferred_element_type=F32))
            ww_half = w0_ref[n:n + 1, :] + _mm(hw, w2_ref[n])
            half_c = -0.5 * math.exp(-0.5)
            lw_ref[n] = half_c * jnp.tanh(ww_half) + half_c
            ha = jnp.dot(xa, a1_ref[n], preferred_element_type=F32)
            aa_half = a0_ref[n:n + 1, :] + _mm(ha, a2_ref[n])
            a_ref[n] = (0.5 * jnp.tanh(aa_half) + 0.5).astype(a_ref.dtype)

    @pl.when(j == 0)
    def _():
        h = _norm_mod(ctx_ref[...], g, modc_ref[...])
        half = d // 2
        prev = jnp.where(row == 0, 0.0, pltpu.roll(h[:, :half], 1, 0))
        nxt = jnp.where(row == t - 1, 0.0, pltpu.roll(h[:, half:], t - 1, 0))
        emit(h, jnp.concatenate([prev, nxt], axis=-1))

    @pl.when(j > 0)
    def _():
        mod = modl_ref[...]
        q = d // 4
        h = _norm_mod(x_ref[...], g, mod)
        hp = _norm_mod(xp_ref[...], g, mod, cols=slice(2 * q, 3 * q))
        hn = _norm_mod(xn_ref[...], g, mod, cols=slice(3 * q, 4 * q))
        col = row % GRID_W
        left = jnp.where(col == 0, 0.0, pltpu.roll(h[:, :q], 1, 0))
        right = jnp.where(col == GRID_W - 1, 0.0, pltpu.roll(h[:, q:2 * q], t - 1, 0))
        up = jnp.concatenate([hp, h[:t - GRID_W, 2 * q:3 * q]], axis=0)
        up = jnp.where((j == 1) & (row < GRID_W), 0.0, up)
        down = jnp.concatenate([h[GRID_W:, 3 * q:], hn], axis=0)
        down = jnp.where((j == n_lat_tiles) & (row >= t - GRID_W), 0.0, down)
        emit(h, jnp.concatenate([left, right, up, down], axis=-1))


def _rk_pro(x, ctx, mod0, norm_g0, mix, w1, w2, a1, a2, w0, a0):
    b, l, d = x.shape
    t = ROW_TILE
    n_lat = l // t
    sub = t // GRID_W
    n_rows = l // GRID_W
    xr = x.reshape(b, n_rows, GRID_W, d)
    nmix = mix.shape[0]
    n_proj = nmix - 2
    r = w1.shape[-1]
    lt = ctx.shape[1] + l
    full = lambda shape: pl.BlockSpec(shape, lambda bi, j: (0,) * len(shape))
    tok2 = pl.BlockSpec((2, None, t, d), lambda bi, j: (0, bi, j, 0))
    return pl.pallas_call(
        functools.partial(_rk_pro_kernel, n_lat_tiles=n_lat),
        out_shape=(jax.ShapeDtypeStruct((n_proj, b, lt, d), BF16),
                   jax.ShapeDtypeStruct((2, b, lt, d), F32),
                   jax.ShapeDtypeStruct((2, b, lt, d), ACT_DTYPE)),
        grid=(b, n_lat + 1),
        in_specs=[
            pl.BlockSpec((None, t, d), lambda bi, j: (bi, 0, 0)),
            pl.BlockSpec((None, t, d), lambda bi, j: (bi, jnp.maximum(j - 1, 0), 0)),
            pl.BlockSpec((None, None, GRID_W, d),
                         lambda bi, j: (bi, jnp.maximum((j - 1) * sub - 1, 0), 0, 0)),
            pl.BlockSpec((None, None, GRID_W, d),
                         lambda bi, j: (bi, jnp.minimum(jnp.maximum(j, 1) * sub, n_rows - 1), 0, 0)),
            pl.BlockSpec((None, 3, d), lambda bi, j: (b, 0, 0)),
            pl.BlockSpec((None, 3, d), lambda bi, j: (bi, 0, 0)),
            pl.BlockSpec((1, d), lambda bi, j: (0, 0)),
            pl.BlockSpec((nmix, d), lambda bi, j: (0, 0)),
            full((2, d, r)), full((2, r, d)), full((2, d, r)), full((2, r, d)),
            full((2, d)), full((2, d)),
        ],
        out_specs=(pl.BlockSpec((n_proj, None, t, d), lambda bi, j: (0, bi, j, 0)), tok2, tok2),
        compiler_params=_cparams(("parallel", "parallel")),
        name="rk_pro",
    )(ctx, x, xr, xr, mod0, mod0, norm_g0.reshape(1, d), mix, w1, w2, a1, a2, w0, a0)


def _rk_proj_kernel(x_ref, w_ref, o_ref):
    o_ref[...] = jnp.dot(x_ref[...], w_ref[...], preferred_element_type=F32).astype(o_ref.dtype)


def _rk_proj(xm, w_in):
    nproj, d, _ = w_in.shape
    m = xm.shape[1]
    tm = RT_PROJ_TILE if m % RT_PROJ_TILE == 0 else MM_TILE
    return pl.pallas_call(
        _rk_proj_kernel,
        out_shape=jax.ShapeDtypeStruct((nproj, m, d), ACT_DTYPE),
        grid=(nproj, m // tm),
        in_specs=[
            pl.BlockSpec((None, tm, d), lambda p, i: (p, i, 0)),
            pl.BlockSpec((None, d, d), lambda p, i: (p, 0, 0)),
        ],
        out_specs=pl.BlockSpec((None, tm, d), lambda p, i: (p, i, 0)),
        compiler_params=_cparams(("parallel", "parallel")),
        name="rk_proj",
    )(xm, w_in)


def _wkv_kernel(r_ref, k_ref, v_ref, lw_ref, a_ref, kk_ref, ka_ref, *rest, reverse, chunks_per_block):
    prev_ref = rest[0] if len(rest) == 3 else None
    o_ref, t_ref = rest[-2:]
    c = WKV_CHUNK
    gw = WKV_GROUP
    nq = r_ref.shape[-1] // gw
    nh = gw // RWKV_HEAD
    hb = gw
    s = pl.program_id(2)

    @pl.when(s == 0)
    def _():
        t_ref[...] = jnp.zeros_like(t_ref)

    row_id = lax.broadcasted_iota(jnp.int32, (c, 1), 0)

    def running_sum(x):
        sh = 1
        while sh < c:
            if reverse:
                x = x + jnp.where(row_id < c - sh, pltpu.roll(x, c - sh, 0), 0.0)
            else:
                x = x + jnp.where(row_id >= sh, pltpu.roll(x, sh, 0), 0.0)
            sh *= 2
        return x

    rw = lax.broadcasted_iota(jnp.int32, (c, nh * c), 0)
    cw = lax.broadcasted_iota(jnp.int32, (c, nh * c), 1) % c
    strict_w = (cw > rw) if reverse else (cw < rw)
    incl_w = strict_w | (cw == rw)
    eye_w = (cw == rw).astype(F32)
    rs = lax.broadcasted_iota(jnp.int32, (nh * c, nh * c), 0)
    cs = lax.broadcasted_iota(jnp.int32, (nh * c, nh * c), 1)
    bd_mask = ((rs // c) == (cs // c)).astype(BF16)
    rl = lax.broadcasted_iota(jnp.int32, (hb, hb), 0)
    cl = lax.broadcasted_iota(jnp.int32, (hb, hb), 1)
    head_bd = (rl // RWKV_HEAD) == (cl // RWKV_HEAD)
    ones_bd = head_bd.astype(BF16)
    lane_head = lax.broadcasted_iota(jnp.int32, (1, hb), 1) // RWKV_HEAD
    lane_masks = [(lane_head == h).astype(BF16) for h in range(nh)]
    end_row = 0 if reverse else c - 1

    def stack(x):
        xb = x.astype(BF16)
        return jnp.concatenate([xb * m for m in lane_masks], axis=0)

    def block_diag(xw):
        return jnp.concatenate([xw.astype(BF16)] * nh, axis=0) * bd_mask

    def prepass(cidx, q, out):
        rows = pl.ds(cidx * c, c)
        lanes = pl.ds(q * gw, gw)
        r = r_ref[rows, lanes].astype(F32)
        k = k_ref[rows, lanes].astype(F32)
        v = v_ref[rows, lanes].astype(F32)
        lw = lw_ref[rows, lanes]
        a = a_ref[rows, lanes].astype(F32)
        kap = ka_ref[:, lanes]

        b = running_sum(lw)
        kk = k * kk_ref[:, lanes]
        ssq = ssq_blocks[q][cidx * c:(cidx + 1) * c]
        b_end = b[end_row:end_row + 1, :]
        e_pos = jnp.exp(b)
        e_neg = jnp.exp(-b)
        e_prev = jnp.exp(b - lw)
        e_end = jnp.exp(b_end - b)
        p_end = jnp.exp(b_end)

        kappa = kk / jnp.maximum(jnp.sqrt(ssq), 1e-12)
        kd = k * (1.0 + (a - 1.0) * kap)
        beta = kappa * a

        kt = kappa * e_prev
        rt = r * e_pos
        bt_s = stack(beta * e_neg)
        kn_s = stack(kd * e_neg)
        v_s = stack(v)
        bbar = beta * e_end
        kbar = kd * e_end

        lhs = jnp.concatenate([kt, rt], axis=0)
        ab = _mm_nt(lhs, bt_s)
        ak = _mm_nt(lhs, kn_s)
        yield
        a_ab = jnp.where(strict_w, ab[:c], 0.0)
        a_rb = jnp.where(incl_w, ab[c:], 0.0)
        a_ak = jnp.where(strict_w, ak[:c], 0.0)
        a_rk = jnp.where(incl_w, ak[c:], 0.0)

        pw = -a_ab
        minv = eye_w + pw
        pw = _mm(pw, block_diag(pw))
        intra_v = _mm(jnp.concatenate([a_ak, a_rk], axis=0), v_s)
        av = intra_v[:c]
        ark_v = intra_v[c:]
        yield
        n_sq = int(math.log2(c)) - 1
        for i in range(n_sq):
            both = _mm(jnp.concatenate([minv, pw], axis=0) if i < n_sq - 1 else minv, block_diag(pw))
            minv = minv + both[:c]
            pw = both[c:]
            yield

        keys = jnp.concatenate([kbar, bbar], axis=0).astype(BF16)
        out[(cidx, q)] = (lhs.astype(BF16), minv.astype(BF16), a_rb.astype(BF16), av, ark_v,
                          v.astype(BF16), keys, p_end)

    order = list(range(chunks_per_block))
    if reverse:
        order.reverse()
    factors = {}
    ssq_blocks = []
    for q in range(nq):
        kk_blk = k_ref[:, pl.ds(q * gw, gw)].astype(F32) * kk_ref[:, pl.ds(q * gw, gw)]
        ssq_blocks.append(_mm(kk_blk * kk_blk, ones_bd))
    def recurrence():
        for cidx in order:
            while any((cidx, q) not in factors for q in range(nq)):
                yield
            rows = pl.ds(cidx * c, c)
            fac = [factors[(cidx, q)] for q in range(nq)]
            reads = [_dot(f[0], t_ref[q].astype(BF16), _NT) for q, f in enumerate(fac)]
            yield
            us = [-_dot(f[1], stack(reads[q][:c] + f[3]), _NN) for q, f in enumerate(fac)]
            yield
            for q, f in enumerate(fac):
                _, _, a_rb, _, ark_v, vb, keys, p_end = f
                o = reads[q][c:] + ark_v + _dot(a_rb, stack(us[q]), _NN)
                if prev_ref is not None:
                    o = o + prev_ref[rows, pl.ds(q * gw, gw)].astype(F32)
                o_ref[rows, pl.ds(q * gw, gw)] = o.astype(o_ref.dtype)
                vu = jnp.concatenate([vb, us[q].astype(BF16)], axis=0)
                t_ref[q] = t_ref[q] * p_end + jnp.where(head_bd, _dot(vu, keys, _TN), 0.0)
            yield

    chains = [prepass(cidx, q, factors) for cidx in order for q in range(nq)]
    start = [(i // nq) * WKV_STAGGER for i in range(len(chains))]
    live = list(range(len(chains)))
    scan = recurrence()
    rnd = 0
    while scan is not None:
        live = [i for i in live if start[i] > rnd or next(chains[i], "done") != "done"]
        if next(scan, "done") == "done":
            scan = None
        rnd += 1


def _wkv(proj, lw, a, k_k, k_a, direction, n_ctx_blocks, prev=None):
    _, b, lt, d = proj.shape
    rb = ROW_TILE
    hb = min(WKV_LANES, d)
    nblk = lt // rb
    reverse = direction == 1

    if reverse:
        def blk(s):
            return jnp.where(s < n_ctx_blocks, n_ctx_blocks - 1 - s, nblk - 1 - (s - n_ctx_blocks))
    else:
        def blk(s):
            return s

    def tok(p):
        return pl.BlockSpec((None, None, rb, hb), lambda bi, hi, s: (p, bi, blk(s), hi))

    out_spec = pl.BlockSpec((None, rb, hb), lambda bi, hi, s: (bi, blk(s), hi))
    extra = () if prev is None else (prev,)
    return pl.pallas_call(
        functools.partial(_wkv_kernel, reverse=reverse, chunks_per_block=rb // WKV_CHUNK),
        out_shape=jax.ShapeDtypeStruct((b, lt, d), ACT_DTYPE),
        grid=(b, d // hb, nblk),
        in_specs=[tok(0), tok(1), tok(2), tok(direction), tok(direction),
                  pl.BlockSpec((1, hb), lambda bi, hi, s: (0, hi)),
                  pl.BlockSpec((1, hb), lambda bi, hi, s: (0, hi))] + [out_spec] * len(extra),
        out_specs=out_spec,
        scratch_shapes=[pltpu.VMEM((hb // WKV_GROUP, WKV_GROUP, WKV_GROUP), F32)],
        compiler_params=_cparams(("parallel", "parallel", "arbitrary")),
        name="wkv_bwd" if reverse else "wkv_fwd",
    )(proj, proj, proj, lw, a, k_k, k_a, *extra)


def _rk_out_kernel(o_ref, r_ref, k_ref, v_ref, g_ref, a_ref, x_ref, ctx_ref,
                   modl_ref, modc_ref, modl1_ref, modc1_ref, ng1_ref,
                   ka_ref, rk_ref, lng_ref, lnb_ref, w_ref,
                   x1_ref, h1_ref, *, n_ctx_tiles):
    is_ctx = pl.program_id(1) < n_ctx_tiles
    gate = jnp.where(is_ctx, modc_ref[2:3, :], modl_ref[2:3, :])
    mod1 = jnp.where(is_ctx, modc1_ref[...], modl1_ref[...])
    ng1 = ng1_ref[...]
    hw = min(MXU_DIM, r_ref.shape[-1])
    rl = lax.broadcasted_iota(jnp.int32, (hw, hw), 0)
    cl = lax.broadcasted_iota(jnp.int32, (hw, hw), 1)
    same_head = (rl // RWKV_HEAD) == (cl // RWKV_HEAD)
    ones_bd = same_head.astype(BF16)
    mean_bd = jnp.where(same_head, 1.0 / RWKV_HEAD, 0.0).astype(BF16)

    groups = [pl.ds(i, hw) for i in range(0, r_ref.shape[-1], hw)]
    f32 = lambda ref, *idx: ref[idx].astype(F32)
    sub = min(OUT_TILE, o_ref.shape[0])
    for r0 in range(0, o_ref.shape[0], sub):
        rows = pl.ds(r0, sub)
        mu = [_mm(o_ref[rows, gs], mean_bd) for gs in groups]
        rkk = []
        for gs in groups:
            ka = ka_ref[:, gs]
            rk = rk_ref[:, gs]
            a_sum = (a_ref[0, rows, gs] + a_ref[1, rows, gs]).astype(F32)
            scale = (2.0 - 2.0 * ka) * rk + a_sum * (ka * rk)
            rkk.append(_mm((r_ref[rows, gs] * k_ref[rows, gs]).astype(F32) * scale, ones_bd))
        oc = [f32(o_ref, rows, gs) - mg for gs, mg in zip(groups, mu)]
        var = [_mm(cg * cg, mean_bd) for cg in oc]
        out = None
        per = max(1, len(groups) // RK_OUT_SPLITS)
        for g0 in range(0, len(groups), per):
            ys = []
            for gi in range(g0, min(g0 + per, len(groups))):
                gs = groups[gi]
                on = oc[gi] * lax.rsqrt(var[gi] + RWKV_GN_EPS) * lng_ref[:, gs] + lnb_ref[:, gs]
                y = (on + rkk[gi] * f32(v_ref, rows, gs)) * _silu(f32(g_ref, rows, gs))
                ys.append(y.astype(BF16))
            part = _dot(jnp.concatenate(ys, axis=-1), w_ref[pl.ds(g0 * hw, len(ys) * hw), :], _NN)
            out = part if out is None else out + part
        new = jnp.where(is_ctx, ctx_ref[rows, :], x_ref[rows, :]) + gate * out
        x1_ref[rows, :] = new
        h1_ref[rows, :] = _norm_mod(new, ng1, mod1).astype(BF16)


def _rk_out(o_sum, proj, a, x, ctx, mod0, mod1, norm_g1, k_a, r_k, ln_g, ln_b, w_out):
    b, l, d = x.shape
    lc = ctx.shape[1]
    lt = lc + l
    t = ROW_TILE
    nct = lc // t
    tok = pl.BlockSpec((None, t, d), lambda bi, j: (bi, j, 0))
    ptok = lambda p: pl.BlockSpec((None, None, t, d), lambda bi, j: (p, bi, j, 0))
    vec = pl.BlockSpec((1, d), lambda bi, j: (0, 0))
    lat_blk = lambda bi, j: (bi, jnp.maximum(j - nct, 0), 0)
    mod_l = pl.BlockSpec((None, 3, d), lambda bi, j: (bi, 0, 0))
    mod_c = pl.BlockSpec((None, 3, d), lambda bi, j: (b, 0, 0))
    return pl.pallas_call(
        functools.partial(_rk_out_kernel, n_ctx_tiles=nct),
        out_shape=(jax.ShapeDtypeStruct((b, l, d), F32),
                   jax.ShapeDtypeStruct((b, lt, d), BF16)),
        grid=(b, lt // t),
        in_specs=[tok, ptok(0), ptok(1), ptok(2), ptok(3),
                  pl.BlockSpec((2, None, t, d), lambda bi, j: (0, bi, j, 0)),
                  pl.BlockSpec((None, t, d), lat_blk),
                  pl.BlockSpec((None, lc, d), lambda bi, j: (bi, 0, 0)),
                  mod_l, mod_c, mod_l, mod_c, vec, vec, vec, vec, vec,
                  pl.BlockSpec((d, d), lambda bi, j: (0, 0))],
        out_specs=(pl.BlockSpec((None, t, d), lat_blk),
                   pl.BlockSpec((None, t, d), lambda bi, j: (bi, j, 0))),
        compiler_params=_cparams(("parallel", "arbitrary")),
        name="rk_out",
    )(o_sum, proj, proj, proj, proj, a, x, ctx, mod0, mod0, mod1, mod1,
      norm_g1.reshape(1, d), k_a, r_k, ln_g, ln_b, w_out)


def _rt_proj_kernel(h_ref, w_ref, cos_ref, sin_ref, o_ref, *, dk):
    n = pl.program_id(0)
    tn = w_ref.shape[-1]
    half = dk // 2
    sub = min(ROW_TILE, h_ref.shape[0])
    chunks = [pl.ds(i, sub) for i in range(0, h_ref.shape[0], sub)]

    def project(rows):
        return jnp.dot(h_ref[rows, :], w_ref[...], preferred_element_type=F32)

    def rotate(scale):
        for rows in chunks:
            acc = project(rows)
            cos = cos_ref[rows, :]
            sin = sin_ref[rows, :]
            parts = []
            for h in range(tn // dk):
                x1 = acc[:, h * dk:h * dk + half]
                x2 = acc[:, h * dk + half:(h + 1) * dk]
                parts.append((x1 * cos - x2 * sin) * scale)
                parts.append((x1 * sin + x2 * cos) * scale)
            o_ref[rows, :] = jnp.concatenate(parts, axis=-1).astype(BF16)

    @pl.when(n == 0)
    def _():
        rotate(1.0)

    @pl.when(n == 1)
    def _():
        rotate(dk ** -0.5)

    @pl.when(n > 1)
    def _():
        for rows in chunks:
            o_ref[rows, :] = project(rows).astype(BF16)


def _rt_proj(h1, w_in, cos, sin, dk):
    m, d = h1.shape
    n_out = w_in.shape[1]
    tn = d
    tm = RT_PROJ_TILE if m % RT_PROJ_TILE == 0 else MM_TILE
    half = dk // 2
    return pl.pallas_call(
        functools.partial(_rt_proj_kernel, dk=dk),
        out_shape=jax.ShapeDtypeStruct((m, n_out), BF16),
        grid=(n_out // tn, m // tm),
        in_specs=[
            pl.BlockSpec((tm, d), lambda n, i: (i, 0)),
            pl.BlockSpec((d, tn), lambda n, i: (0, n)),
            pl.BlockSpec((tm, half), lambda n, i: (i, 0)),
            pl.BlockSpec((tm, half), lambda n, i: (i, 0)),
        ],
        out_specs=pl.BlockSpec((tm, tn), lambda n, i: (i, n)),
        compiler_params=_cparams(("parallel", "parallel")),
        name="rt_proj",
    )(h1, w_in, cos, sin)


def _log_sigmoid(x):
    return jnp.minimum(x, 0.0) - jnp.log1p(jnp.exp(-jnp.abs(x)))


def _ret_bstate_kernel(lg_ref, k_ref, v_ref, rb_ref, r_ref, *, dk, dv):
    c = k_ref.shape[0]
    s = pl.program_id(1)

    @pl.when(s == 0)
    def _():
        r_ref[...] = jnp.zeros_like(r_ref)

    pos = lax.broadcasted_iota(jnp.int32, (c, 1), 0).astype(F32)
    for h in range(RET_HEADS):
        lgb = _log_sigmoid(lg_ref[1, h])[:, 0:1]
        k = k_ref[:, h * dk:(h + 1) * dk]
        v = v_ref[:, h * dv:(h + 1) * dv]
        r0 = r_ref[h]
        rb_ref[h] = r0.astype(BF16)
        kd = (k.astype(F32) * jnp.exp(pos * lgb)).astype(BF16)
        r_ref[h] = r0 * jnp.exp(float(c) * lgb) + _dot(kd, v, _TN)


def _ret_main_kernel(lg_ref, q_ref, k_ref, v_ref, rb_ref, gng_ref, y_ref, rf_ref, dm_ref, dvec_ref,
                     *, dk, dv, n_ctx_blocks):
    c = q_ref.shape[0]
    s = pl.program_id(1)

    @pl.when(s == 0)
    def _():
        rf_ref[...] = jnp.zeros_like(rf_ref)
        diff = (lax.broadcasted_iota(jnp.int32, (c, c), 0)
                - lax.broadcasted_iota(jnp.int32, (c, c), 1)).astype(F32)
        pos = lax.broadcasted_iota(jnp.int32, (c, LANE), 0).astype(F32)
        for h in range(RET_HEADS):
            lgf = _log_sigmoid(lg_ref[0, h])
            lgb = _log_sigmoid(lg_ref[1, h])
            dm_ref[h] = jnp.where(diff >= 0, jnp.exp(jnp.maximum(diff, 0.0) * lgf[:, 0:1]),
                                  jnp.exp(jnp.maximum(-diff, 0.0) * lgb[:, 0:1]))
            dvec_ref[h, 0] = jnp.exp((pos + 1.0) * lgf)
            dvec_ref[h, 1] = jnp.exp((float(c) - pos) * lgb)
            dvec_ref[h, 2] = jnp.exp((float(c) - 1.0 - pos) * lgf)

    def lanes(x, width):
        return x[:, :width] if width <= LANE else jnp.concatenate([x] * (width // LANE), axis=1)

    @pl.when(s >= n_ctx_blocks)
    def _():
        scores = []
        for h in range(RET_HEADS):
            q = q_ref[:, h * dk:(h + 1) * dk]
            k = k_ref[:, h * dk:(h + 1) * dk]
            scores.append(_dot(q, k, _NT))
        outs = []
        for h in range(RET_HEADS):
            q = q_ref[:, h * dk:(h + 1) * dk]
            sc = (scores[h] * dm_ref[h]).astype(BF16)
            lhs = jnp.concatenate([sc, q * lanes(dvec_ref[h, 0], dk).astype(BF16),
                                   q * lanes(dvec_ref[h, 1], dk).astype(BF16)], axis=1)
            rhs = jnp.concatenate([v_ref[:, h * dv:(h + 1) * dv], rf_ref[h].astype(BF16), rb_ref[h]],
                                  axis=0)
            outs.append(_dot(lhs, rhs, _NN))
        for h in range(RET_HEADS):
            o = outs[h]
            on = o * lax.rsqrt(jnp.mean(o * o, axis=-1, keepdims=True) + EPS)
            y_ref[:, h * dv:(h + 1) * dv] = (on * gng_ref[:, h * dv:(h + 1) * dv]).astype(BF16)

    for h in range(RET_HEADS):
        k = k_ref[:, h * dk:(h + 1) * dk]
        v = v_ref[:, h * dv:(h + 1) * dv]
        kd = (k.astype(F32) * lanes(dvec_ref[h, 2], dk)).astype(BF16)
        gamma_c = jnp.exp(float(c) * _log_sigmoid(lg_ref[0, h]))
        rf_ref[h] = rf_ref[h] * lanes(gamma_c, dv) + _dot(kd, v, _TN)


def _ret(qkvg, decay_logit, gn_g, n_ctx_blocks, dk, dv):
    b, lt, _ = qkvg.shape
    c = RET_CHUNK
    nblk = lt // c
    nh = RET_HEADS
    d = nh * dk
    vd = nh * dv
    assert (2 * d) % vd == 0
    voff = (2 * d) // vd
    lg = jnp.broadcast_to(decay_logit.astype(F32)[:, :, None, None], (2, nh, 1, LANE))
    lg_spec = pl.BlockSpec((2, nh, 1, LANE), lambda bi, s: (0, 0, 0, 0))

    def bwd_blk(s):
        return jnp.where(s < n_ctx_blocks, n_ctx_blocks - 1 - s, nblk - 1 - (s - n_ctx_blocks))

    r_b = pl.pallas_call(
        functools.partial(_ret_bstate_kernel, dk=dk, dv=dv),
        out_shape=jax.ShapeDtypeStruct((b, nblk, nh, dk, dv), BF16),
        grid=(b, nblk),
        in_specs=[
            lg_spec,
            pl.BlockSpec((None, c, d), lambda bi, s: (bi, bwd_blk(s), 1)),
            pl.BlockSpec((None, c, vd), lambda bi, s: (bi, bwd_blk(s), voff)),
        ],
        out_specs=pl.BlockSpec((None, None, nh, dk, dv), lambda bi, s: (bi, bwd_blk(s), 0, 0, 0)),
        scratch_shapes=[pltpu.VMEM((nh, dk, dv), F32)],
        compiler_params=_cparams(("parallel", "arbitrary")),
        name="ret_bstate",
    )(lg, qkvg, qkvg)

    lat = lambda bi, s: (bi, jnp.maximum(s - n_ctx_blocks, 0), 0)
    return pl.pallas_call(
        functools.partial(_ret_main_kernel, dk=dk, dv=dv, n_ctx_blocks=n_ctx_blocks),
        out_shape=jax.ShapeDtypeStruct((b, lt - n_ctx_blocks * c, vd), BF16),
        grid=(b, nblk),
        in_specs=[
            lg_spec,
            pl.BlockSpec((None, c, d), lambda bi, s: (bi, s, 0)),
            pl.BlockSpec((None, c, d), lambda bi, s: (bi, s, 1)),
            pl.BlockSpec((None, c, vd), lambda bi, s: (bi, s, voff)),
            pl.BlockSpec((None, None, nh, dk, dv), lambda bi, s: (bi, s, 0, 0, 0)),
            pl.BlockSpec((1, vd), lambda bi, s: (0, 0)),
        ],
        out_specs=pl.BlockSpec((None, c, vd), lat),
        scratch_shapes=[pltpu.VMEM((nh, dk, dv), F32), pltpu.VMEM((nh, c, c), F32),
                        pltpu.VMEM((nh, 3, c, LANE), F32)],
        compiler_params=_cparams(("parallel", "arbitrary")),
        name="ret_main",
    )(lg, qkvg, qkvg, qkvg, r_b, gn_g.reshape(1, vd))


def _rt_out_kernel(y_ref, g_ref, x_ref, mod_ref, fg_ref, w_ref, o_ref):
    gated = y_ref[...].astype(F32) * _silu(g_ref[...].astype(F32))
    new = x_ref[...] + mod_ref[2:3, :] * _mm(gated, w_ref[...])
    o_ref[...] = new * lax.rsqrt(jnp.mean(new * new, axis=-1, keepdims=True) + EPS) * fg_ref[...]


def _rt_out(y, qkvg, x1, mod1, final_g, w_out, n_ctx_tiles):
    b, l, d = x1.shape
    vd = w_out.shape[0]
    t = ROW_TILE
    goff = (qkvg.shape[-1] - vd) // vd
    return pl.pallas_call(
        _rt_out_kernel,
        out_shape=jax.ShapeDtypeStruct((b, l, d), F32),
        grid=(b, l // t),
        in_specs=[
            pl.BlockSpec((None, t, vd), lambda bi, j: (bi, j, 0)),
            pl.BlockSpec((None, t, vd), lambda bi, j: (bi, j + n_ctx_tiles, goff)),
            pl.BlockSpec((None, t, d), lambda bi, j: (bi, j, 0)),
            pl.BlockSpec((None, 3, d), lambda bi, j: (bi, 0, 0)),
            pl.BlockSpec((1, d), lambda bi, j: (0, 0)),
            pl.BlockSpec((vd, d), lambda bi, j: (0, 0)),
        ],
        out_specs=pl.BlockSpec((None, t, d), lambda bi, j: (bi, j, 0)),
        compiler_params=_cparams(("parallel", "parallel")),
        name="rt_out",
    )(y, qkvg, x1, mod1, final_g.reshape(1, d), w_out)


def _rope_tables(l, lc, dk):
    t = jnp.arange(l)
    row = (t // GRID_W).astype(F32)
    col = (t % GRID_W).astype(F32)
    nf = dk // 4
    inv = ROPE_BASE ** (-jnp.arange(nf, dtype=F32) / nf)
    ang = jnp.concatenate([row[:, None] * inv, col[:, None] * inv], axis=-1)
    cos = jnp.concatenate([jnp.ones((lc, dk // 2), F32), jnp.cos(ang)], axis=0)
    sin = jnp.concatenate([jnp.zeros((lc, dk // 2), F32), jnp.sin(ang)], axis=0)
    return cos, sin


def kernel(x, c, ctx, c_ctx, ada_w, ada_b, norm_g, rk_mix, rk_w_in, rk_w0, rk_w1, rk_w2, rk_a0, rk_a1,
           rk_a2, rk_k_k, rk_k_a, rk_r_k, rk_ln_g, rk_ln_b, rk_w_out, rt_w_in, rt_decay_logit, rt_gn_g,
           rt_w_out, final_g):
    b, l, d = x.shape
    lc = ctx.shape[1]
    lt = lc + l
    assert lc == ROW_TILE and l % ROW_TILE == 0 and ROW_TILE % GRID_W == 0
    assert d % min(WKV_LANES, d) == 0 and d % WKV_GROUP == 0 and (b * lt) % MM_TILE == 0
    assert ada_w.shape[0] == 2 and rk_w_in.shape[0] == 1 and rt_w_in.shape[0] == 1
    dk = d // RET_HEADS
    dv = rt_w_out.shape[1] // RET_HEADS

    n_mod = -(-(b + 1) // 8) * 8
    cv = jnp.concatenate([c, c_ctx[None], jnp.zeros((n_mod - b - 1, d), F32)], axis=0)
    mod = _adaln(cv, ada_w, ada_b).reshape(2, n_mod, 3, d)

    xm, lw, a = _rk_pro(x, ctx, mod[0], norm_g[0], rk_mix[0], rk_w1[0].astype(BF16),
                        (0.5 * rk_w2[0]).astype(BF16), rk_a1[0].astype(BF16),
                        (0.5 * rk_a2[0]).astype(BF16), 0.5 * rk_w0[0], 0.5 * rk_a0[0])
    proj = _rk_proj(xm.reshape(4, b * lt, d), rk_w_in[0].astype(BF16)).reshape(4, b, lt, d)
    k_k = rk_k_k[0].reshape(1, d)
    k_a = rk_k_a[0].reshape(1, d)
    n_ctx_blocks = lc // ROW_TILE
    o_f = _wkv(proj, lw, a, k_k, k_a, 0, n_ctx_blocks)
    o_sum = _wkv(proj, lw, a, k_k, k_a, 1, n_ctx_blocks, prev=o_f)
    x1, h1 = _rk_out(o_sum, proj, a, x, ctx, mod[0], mod[1], norm_g[1], k_a,
                     rk_r_k[0].reshape(1, d), rk_ln_g[0].reshape(1, d), rk_ln_b[0].reshape(1, d),
                     rk_w_out[0].astype(BF16))

    cos, sin = _rope_tables(l, lc, dk)
    qkvg = _rt_proj(h1.reshape(b * lt, d), rt_w_in[0].astype(BF16), jnp.tile(cos, (b, 1)),
                    jnp.tile(sin, (b, 1)), dk)
    y = _ret(qkvg.reshape(b, lt, -1), rt_decay_logit[0], rt_gn_g[0], lc // RET_CHUNK, dk, dv)
    return _rt_out(y, qkvg.reshape(b, lt, -1), x1, mod[1], final_g, rt_w_out[0].astype(BF16),
                   lc // ROW_TILE)
```

```python
import functools
import math

import jax
import jax.numpy as jnp
from jax import lax
from jax.experimental import pallas as pl
from jax.experimental.pallas import tpu as pltpu

F32 = jnp.float32
BF16 = jnp.bfloat16

GRID_W = 64
EPS = 1e-6
RWKV_HEAD = 64
RWKV_GN_EPS = 64e-5
RET_HEADS = 8
ROPE_BASE = 10000.0

LANE = 128
MXU_DIM = 256
VMEM_LIMIT = 56 * 1024 * 1024

ROW_TILE = 256
OUT_TILE = 128
RK_OUT_SPLITS = 2
MM_TILE = 512
RT_PROJ_TILE = 1024
WKV_CHUNK = 64
WKV_GROUP = MXU_DIM
WKV_STAGGER = 3
WKV_LANES = 2048
RET_CHUNK = 256
ACT_DTYPE = BF16


def _cparams(sem, vmem=VMEM_LIMIT):
    return pltpu.CompilerParams(dimension_semantics=sem, vmem_limit_bytes=vmem)


def _mm(a, b):
    return jnp.dot(a.astype(BF16), b.astype(BF16), preferred_element_type=F32)


def _mm_nt(a, b):
    return lax.dot_general(a.astype(BF16), b.astype(BF16), (((1,), (1,)), ((), ())),
                           preferred_element_type=F32)


def _split2(x):
    hi = x.astype(BF16)
    lo = (x - hi.astype(F32)).astype(BF16)
    return hi, lo


def _dot(a, b, dims):
    return lax.dot_general(a, b, dims, preferred_element_type=F32)


_NN = (((1,), (0,)), ((), ()))
_NT = (((1,), (1,)), ((), ()))
_TN = (((0,), (0,)), ((), ()))


def _mm3(a, b, dims=_NN):
    a1, a2 = _split2(a)
    b1, b2 = _split2(b)
    return _dot(a1, b1, dims) + (_dot(a1, b2, dims) + _dot(a2, b1, dims))


def _silu(x):
    h = 0.5 * x
    return h + h * jnp.tanh(h)


def _norm_mod(xb, g, mod, cols=None):
    inv = lax.rsqrt(jnp.mean(xb * xb, axis=-1, keepdims=True) + EPS)
    gain = g * (1.0 + mod[1:2])
    shift = mod[0:1]
    if cols is not None:
        xb, gain, shift = xb[:, cols], gain[:, cols], shift[:, cols]
    return xb * inv * gain + shift


def _adaln_kernel(cv_ref, w_ref, b_ref, o_ref):
    s = _silu(cv_ref[...])
    o_ref[...] = _mm3(s, w_ref[...]) + b_ref[...]


def _adaln(cv, ada_w, ada_b):
    depth, d, d3 = ada_w.shape
    rows = cv.shape[0]
    tn = math.gcd(d3, 6 * LANE)
    return pl.pallas_call(
        _adaln_kernel,
        out_shape=jax.ShapeDtypeStruct((depth, rows, d3), F32),
        grid=(depth, d3 // tn),
        in_specs=[
            pl.BlockSpec((rows, d), lambda i, n: (0, 0)),
            pl.BlockSpec((None, d, tn), lambda i, n: (i, 0, n)),
            pl.BlockSpec((None, 1, tn), lambda i, n: (i, 0, n)),
        ],
        out_specs=pl.BlockSpec((None, rows, tn), lambda i, n: (i, 0, n)),
        compiler_params=_cparams(("parallel", "parallel")),
        name="adaln",
    )(cv, ada_w, ada_b.reshape(depth, 1, d3))


def _rk_pro_kernel(ctx_ref, x_ref, xp_ref, xn_ref, modc_ref, modl_ref, g_ref, mix_ref,
                   w1_ref, w2_ref, a1_ref, a2_ref, w0_ref, a0_ref, o_ref, lw_ref, a_ref,
                   *, n_lat_tiles):
    j = pl.program_id(1)
    g = g_ref[...]
    mix = mix_ref[...]
    t, d = x_ref.shape
    row = lax.broadcasted_iota(jnp.int32, (t, 1), 0)
    n_proj = o_ref.shape[0]

    def emit(h, hs):
        delta = hs - h
        for p in range(n_proj):
            o_ref[p] = (h + delta * mix[p:p + 1]).astype(BF16)
        xw = (h + delta * mix[n_proj:n_proj + 1]).astype(BF16)
        xa = (h + delta * mix[n_proj + 1:n_proj + 2]).astype(BF16)
        for n in range(2):
            hw = jnp.tanh(jnp.dot(xw, w1_ref[n], preferred_element_type=F32))
            ww_half = w0_ref[n:n + 1, :] + _mm(hw, w2_ref[n])
            half_c = -0.5 * math.exp(-0.5)
            lw_ref[n] = half_c * jnp.tanh(ww_half) + half_c
            ha = jnp.dot(xa, a1_ref[n], preferred_element_type=F32)
            aa_half = a0_ref[n:n + 1, :] + _mm(ha, a2_ref[n])
            a_ref[n] = (0.5 * jnp.tanh(aa_half) + 0.5).astype(a_ref.dtype)

    @pl.when(j == 0)
    def _():
        h = _norm_mod(ctx_ref[...], g, modc_ref[...])
        half = d // 2
        prev = jnp.where(row == 0, 0.0, pltpu.roll(h[:, :half], 1, 0))
        nxt = jnp.where(row == t - 1, 0.0, pltpu.roll(h[:, half:], t - 1, 0))
        emit(h, jnp.concatenate([prev, nxt], axis=-1))

    @pl.when(j > 0)
    def _():
        mod = modl_ref[...]
        q = d // 4
        h = _norm_mod(x_ref[...], g, mod)
        hp = _norm_mod(xp_ref[...], g, mod, cols=slice(2 * q, 3 * q))
        hn = _norm_mod(xn_ref[...], g, mod, cols=slice(3 * q, 4 * q))
        col = row % GRID_W
        left = jnp.where(col == 0, 0.0, pltpu.roll(h[:, :q], 1, 0))
        right = jnp.where(col == GRID_W - 1, 0.0, pltpu.roll(h[:, q:2 * q], t - 1, 0))
        up = jnp.concatenate([hp, h[:t - GRID_W, 2 * q:3 * q]], axis=0)
        up = jnp.where((j == 1) & (row < GRID_W), 0.0, up)
        down = jnp.concatenate([h[GRID_W:, 3 * q:], hn], axis=0)
        down = jnp.where((j == n_lat_tiles) & (row >= t - GRID_W), 0.0, down)
        emit(h, jnp.concatenate([left, right, up, down], axis=-1))


def _rk_pro(x, ctx, mod0, norm_g0, mix, w1, w2, a1, a2, w0, a0):
    b, l, d = x.shape
    t = ROW_TILE
    n_lat = l // t
    sub = t // GRID_W
    n_rows = l // GRID_W
    xr = x.reshape(b, n_rows, GRID_W, d)
    nmix = mix.shape[0]
    n_proj = nmix - 2
    r = w1.shape[-1]
    lt = ctx.shape[1] + l
    full = lambda shape: pl.BlockSpec(shape, lambda bi, j: (0,) * len(shape))
    tok2 = pl.BlockSpec((2, None, t, d), lambda bi, j: (0, bi, j, 0))
    return pl.pallas_call(
        functools.partial(_rk_pro_kernel, n_lat_tiles=n_lat),
        out_shape=(jax.ShapeDtypeStruct((n_proj, b, lt, d), BF16),
                   jax.ShapeDtypeStruct((2, b, lt, d), F32),
                   jax.ShapeDtypeStruct((2, b, lt, d), ACT_DTYPE)),
        grid=(b, n_lat + 1),
        in_specs=[
            pl.BlockSpec((None, t, d), lambda bi, j: (bi, 0, 0)),
            pl.BlockSpec((None, t, d), lambda bi, j: (bi, jnp.maximum(j - 1, 0), 0)),
            pl.BlockSpec((None, None, GRID_W, d),
                         lambda bi, j: (bi, jnp.maximum((j - 1) * sub - 1, 0), 0, 0)),
            pl.BlockSpec((None, None, GRID_W, d),
                         lambda bi, j: (bi, jnp.minimum(jnp.maximum(j, 1) * sub, n_rows - 1), 0, 0)),
            pl.BlockSpec((None, 3, d), lambda bi, j: (b, 0, 0)),
            pl.BlockSpec((None, 3, d), lambda bi, j: (bi, 0, 0)),
            pl.BlockSpec((1, d), lambda bi, j: (0, 0)),
            pl.BlockSpec((nmix, d), lambda bi, j: (0, 0)),
            full((2, d, r)), full((2, r, d)), full((2, d, r)), full((2, r, d)),
            full((2, d)), full((2, d)),
        ],
        out_specs=(pl.BlockSpec((n_proj, None, t, d), lambda bi, j: (0, bi, j, 0)), tok2, tok2),
        compiler_params=_cparams(("parallel", "parallel")),
        name="rk_pro",
    )(ctx, x, xr, xr, mod0, mod0, norm_g0.reshape(1, d), mix, w1, w2, a1, a2, w0, a0)


def _rk_proj_kernel(x_ref, w_ref, o_ref):
    o_ref[...] = jnp.dot(x_ref[...], w_ref[...], preferred_element_type=F32).astype(o_ref.dtype)


def _rk_proj(xm, w_in):
    nproj, d, _ = w_in.shape
    m = xm.shape[1]
    tm = RT_PROJ_TILE if m % RT_PROJ_TILE == 0 else MM_TILE
    return pl.pallas_call(
        _rk_proj_kernel,
        out_shape=jax.ShapeDtypeStruct((nproj, m, d), ACT_DTYPE),
        grid=(nproj, m // tm),
        in_specs=[
            pl.BlockSpec((None, tm, d), lambda p, i: (p, i, 0)),
            pl.BlockSpec((None, d, d), lambda p, i: (p, 0, 0)),
        ],
        out_specs=pl.BlockSpec((None, tm, d), lambda p, i: (p, i, 0)),
        compiler_params=_cparams(("parallel", "parallel")),
        name="rk_proj",
    )(xm, w_in)


def _wkv_kernel(r_ref, k_ref, v_ref, lw_ref, a_ref, kk_ref, ka_ref, *rest, reverse, chunks_per_block):
    prev_ref = rest[0] if len(rest) == 3 else None
    o_ref, t_ref = rest[-2:]
    c = WKV_CHUNK
    gw = WKV_GROUP
    nq = r_ref.shape[-1] // gw
    nh = gw // RWKV_HEAD
    hb = gw
    s = pl.program_id(2)

    @pl.when(s == 0)
    def _():
        t_ref[...] = jnp.zeros_like(t_ref)

    row_id = lax.broadcasted_iota(jnp.int32, (c, 1), 0)

    def running_sum(x):
        sh = 1
        while sh < c:
            if reverse:
                x = x + jnp.where(row_id < c - sh, pltpu.roll(x, c - sh, 0), 0.0)
            else:
                x = x + jnp.where(row_id >= sh, pltpu.roll(x, sh, 0), 0.0)
            sh *= 2
        return x

    rw = lax.broadcasted_iota(jnp.int32, (c, nh * c), 0)
    cw = lax.broadcasted_iota(jnp.int32, (c, nh * c), 1) % c
    strict_w = (cw > rw) if reverse else (cw < rw)
    incl_w = strict_w | (cw == rw)
    eye_w = (cw == rw).astype(F32)
    rs = lax.broadcasted_iota(jnp.int32, (nh * c, nh * c), 0)
    cs = lax.broadcasted_iota(jnp.int32, (nh * c, nh * c), 1)
    bd_mask = ((rs // c) == (cs // c)).astype(BF16)
    rl = lax.broadcasted_iota(jnp.int32, (hb, hb), 0)
    cl = lax.broadcasted_iota(jnp.int32, (hb, hb), 1)
    head_bd = (rl // RWKV_HEAD) == (cl // RWKV_HEAD)
    ones_bd = head_bd.astype(BF16)
    lane_head = lax.broadcasted_iota(jnp.int32, (1, hb), 1) // RWKV_HEAD
    lane_masks = [(lane_head == h).astype(BF16) for h in range(nh)]
    end_row = 0 if reverse else c - 1

    def stack(x):
        xb = x.astype(BF16)
        return jnp.concatenate([xb * m for m in lane_masks], axis=0)

    def block_diag(xw):
        return jnp.concatenate([xw.astype(BF16)] * nh, axis=0) * bd_mask

    def prepass(cidx, q, out):
        rows = pl.ds(cidx * c, c)
        lanes = pl.ds(q * gw, gw)
        r = r_ref[rows, lanes].astype(F32)
        k = k_ref[rows, lanes].astype(F32)
        v = v_ref[rows, lanes].astype(F32)
        lw = lw_ref[rows, lanes]
        a = a_ref[rows, lanes].astype(F32)
        kap = ka_ref[:, lanes]

        b = running_sum(lw)
        kk = k * kk_ref[:, lanes]
        ssq = ssq_blocks[q][cidx * c:(cidx + 1) * c]
        b_end = b[end_row:end_row + 1, :]
        e_pos = jnp.exp(b)
        e_neg = jnp.exp(-b)
        e_prev = jnp.exp(b - lw)
        e_end = jnp.exp(b_end - b)
        p_end = jnp.exp(b_end)

        kappa = kk / jnp.maximum(jnp.sqrt(ssq), 1e-12)
        kd = k * (1.0 + (a - 1.0) * kap)
        beta = kappa * a

        kt = kappa * e_prev
        rt = r * e_pos
        bt_s = stack(beta * e_neg)
        kn_s = stack(kd * e_neg)
        v_s = stack(v)
        bbar = beta * e_end
        kbar = kd * e_end

        lhs = jnp.concatenate([kt, rt], axis=0)
        ab = _mm_nt(lhs, bt_s)
        ak = _mm_nt(lhs, kn_s)
        yield
        a_ab = jnp.where(strict_w, ab[:c], 0.0)
        a_rb = jnp.where(incl_w, ab[c:], 0.0)
        a_ak = jnp.where(strict_w, ak[:c], 0.0)
        a_rk = jnp.where(incl_w, ak[c:], 0.0)

        pw = -a_ab
        minv = eye_w + pw
        pw = _mm(pw, block_diag(pw))
        intra_v = _mm(jnp.concatenate([a_ak, a_rk], axis=0), v_s)
        av = intra_v[:c]
        ark_v = intra_v[c:]
        yield
        n_sq = int(math.log2(c)) - 1
        for i in range(n_sq):
            both = _mm(jnp.concatenate([minv, pw], axis=0) if i < n_sq - 1 else minv, block_diag(pw))
            minv = minv + both[:c]
            pw = both[c:]
            yield

        keys = jnp.concatenate([kbar, bbar], axis=0).astype(BF16)
        out[(cidx, q)] = (lhs.astype(BF16), minv.astype(BF16), a_rb.astype(BF16), av, ark_v,
                          v.astype(BF16), keys, p_end)

    order = list(range(chunks_per_block))
    if reverse:
        order.reverse()
    factors = {}
    ssq_blocks = []
    for q in range(nq):
        kk_blk = k_ref[:, pl.ds(q * gw, gw)].astype(F32) * kk_ref[:, pl.ds(q * gw, gw)]
        ssq_blocks.append(_mm(kk_blk * kk_blk, ones_bd))
    def recurrence():
        for cidx in order:
            while any((cidx, q) not in factors for q in range(nq)):
                yield
            rows = pl.ds(cidx * c, c)
            fac = [factors[(cidx, q)] for q in range(nq)]
            reads = [_dot(f[0], t_ref[q].astype(BF16), _NT) for q, f in enumerate(fac)]
            yield
            us = [-_dot(f[1], stack(reads[q][:c] + f[3]), _NN) for q, f in enumerate(fac)]
            yield
            for q, f in enumerate(fac):
                _, _, a_rb, _, ark_v, vb, keys, p_end = f
                o = reads[q][c:] + ark_v + _dot(a_rb, stack(us[q]), _NN)
                if prev_ref is not None:
                    o = o + prev_ref[rows, pl.ds(q * gw, gw)].astype(F32)
                o_ref[rows, pl.ds(q * gw, gw)] = o.astype(o_ref.dtype)
                vu = jnp.concatenate([vb, us[q].astype(BF16)], axis=0)
                t_ref[q] = t_ref[q] * p_end + jnp.where(head_bd, _dot(vu, keys, _TN), 0.0)
            yield

    chains = [prepass(cidx, q, factors) for cidx in order for q in range(nq)]
    start = [(i // nq) * WKV_STAGGER for i in range(len(chains))]
    live = list(range(len(chains)))
    scan = recurrence()
    rnd = 0
    while scan is not None:
        live = [i for i in live if start[i] > rnd or next(chains[i], "done") != "done"]
        if next(scan, "done") == "done":
            scan = None
        rnd += 1


def _wkv(proj, lw, a, k_k, k_a, direction, n_ctx_blocks, prev=None):
    _, b, lt, d = proj.shape
    rb = ROW_TILE
    hb = min(WKV_LANES, d)
    nblk = lt // rb
    reverse = direction == 1

    if reverse:
        def blk(s):
            return jnp.where(s < n_ctx_blocks, n_ctx_blocks - 1 - s, nblk - 1 - (s - n_ctx_blocks))
    else:
        def blk(s):
            return s

    def tok(p):
        return pl.BlockSpec((None, None, rb, hb), lambda bi, hi, s: (p, bi, blk(s), hi))

    out_spec = pl.BlockSpec((None, rb, hb), lambda bi, hi, s: (bi, blk(s), hi))
    extra = () if prev is None else (prev,)
    return pl.pallas_call(
        functools.partial(_wkv_kernel, reverse=reverse, chunks_per_block=rb // WKV_CHUNK),
        out_shape=jax.ShapeDtypeStruct((b, lt, d), ACT_DTYPE),
        grid=(b, d // hb, nblk),
        in_specs=[tok(0), tok(1), tok(2), tok(direction), tok(direction),
                  pl.BlockSpec((1, hb), lambda bi, hi, s: (0, hi)),
                  pl.BlockSpec((1, hb), lambda bi, hi, s: (0, hi))] + [out_spec] * len(extra),
        out_specs=out_spec,
        scratch_shapes=[pltpu.VMEM((hb // WKV_GROUP, WKV_GROUP, WKV_GROUP), F32)],
        compiler_params=_cparams(("parallel", "parallel", "arbitrary")),
        name="wkv_bwd" if reverse else "wkv_fwd",
    )(proj, proj, proj, lw, a, k_k, k_a, *extra)


def _rk_out_kernel(o_ref, r_ref, k_ref, v_ref, g_ref, a_ref, x_ref, ctx_ref,
                   modl_ref, modc_ref, modl1_ref, modc1_ref, ng1_ref,
                   ka_ref, rk_ref, lng_ref, lnb_ref, w_ref,
                   x1_ref, h1_ref, *, n_ctx_tiles):
    is_ctx = pl.program_id(1) < n_ctx_tiles
    gate = jnp.where(is_ctx, modc_ref[2:3, :], modl_ref[2:3, :])
    mod1 = jnp.where(is_ctx, modc1_ref[...], modl1_ref[...])
    ng1 = ng1_ref[...]
    hw = min(MXU_DIM, r_ref.shape[-1])
    rl = lax.broadcasted_iota(jnp.int32, (hw, hw), 0)
    cl = lax.broadcasted_iota(jnp.int32, (hw, hw), 1)
    same_head = (rl // RWKV_HEAD) == (cl // RWKV_HEAD)
    ones_bd = same_head.astype(BF16)
    mean_bd = jnp.where(same_head, 1.0 / RWKV_HEAD, 0.0).astype(BF16)

    groups = [pl.ds(i, hw) for i in range(0, r_ref.shape[-1], hw)]
    f32 = lambda ref, *idx: ref[idx].astype(F32)
    sub = min(OUT_TILE, o_ref.shape[0])

    def sub_tile(r0):
        rows = pl.ds(r0, sub)
        mu = [_mm(o_ref[rows, gs], mean_bd) for gs in groups]
        rkk = []
        for gs in groups:
            ka = ka_ref[:, gs]
            rk = rk_ref[:, gs]
            a_sum = (a_ref[0, rows, gs] + a_ref[1, rows, gs]).astype(F32)
            scale = (2.0 - 2.0 * ka) * rk + a_sum * (ka * rk)
            rkk.append(_mm((r_ref[rows, gs] * k_ref[rows, gs]).astype(F32) * scale, ones_bd))
        yield
        oc = [f32(o_ref, rows, gs) - mg for gs, mg in zip(groups, mu)]
        var = [_mm(cg * cg, mean_bd) for cg in oc]
        yield
        out = None
        per = max(1, len(groups) // RK_OUT_SPLITS)
        for g0 in range(0, len(groups), per):
            ys = []
            for gi in range(g0, min(g0 + per, len(groups))):
                gs = groups[gi]
                on = oc[gi] * lax.rsqrt(var[gi] + RWKV_GN_EPS) * lng_ref[:, gs] + lnb_ref[:, gs]
                y = (on + rkk[gi] * f32(v_ref, rows, gs)) * _silu(f32(g_ref, rows, gs))
                ys.append(y.astype(BF16))
            part = _dot(jnp.concatenate(ys, axis=-1), w_ref[pl.ds(g0 * hw, len(ys) * hw), :], _NN)
            out = part if out is None else out + part
        yield
        new = jnp.where(is_ctx, ctx_ref[rows, :], x_ref[rows, :]) + gate * out
        x1_ref[rows, :] = new
        h1_ref[rows, :] = _norm_mod(new, ng1, mod1).astype(BF16)

    pending = [sub_tile(r0) for r0 in range(0, o_ref.shape[0], sub)]
    while pending:
        pending = [gen for gen in pending if next(gen, "done") != "done"]


def _rk_out(o_sum, proj, a, x, ctx, mod0, mod1, norm_g1, k_a, r_k, ln_g, ln_b, w_out):
    b, l, d = x.shape
    lc = ctx.shape[1]
    lt = lc + l
    t = ROW_TILE
    nct = lc // t
    tok = pl.BlockSpec((None, t, d), lambda bi, j: (bi, j, 0))
    ptok = lambda p: pl.BlockSpec((None, None, t, d), lambda bi, j: (p, bi, j, 0))
    vec = pl.BlockSpec((1, d), lambda bi, j: (0, 0))
    lat_blk = lambda bi, j: (bi, jnp.maximum(j - nct, 0), 0)
    mod_l = pl.BlockSpec((None, 3, d), lambda bi, j: (bi, 0, 0))
    mod_c = pl.BlockSpec((None, 3, d), lambda bi, j: (b, 0, 0))
    return pl.pallas_call(
        functools.partial(_rk_out_kernel, n_ctx_tiles=nct),
        out_shape=(jax.ShapeDtypeStruct((b, l, d), F32),
                   jax.ShapeDtypeStruct((b, lt, d), BF16)),
        grid=(b, lt // t),
        in_specs=[tok, ptok(0), ptok(1), ptok(2), ptok(3),
                  pl.BlockSpec((2, None, t, d), lambda bi, j: (0, bi, j, 0)),
                  pl.BlockSpec((None, t, d), lat_blk),
                  pl.BlockSpec((None, lc, d), lambda bi, j: (bi, 0, 0)),
                  mod_l, mod_c, mod_l, mod_c, vec, vec, vec, vec, vec,
                  pl.BlockSpec((d, d), lambda bi, j: (0, 0))],
        out_specs=(pl.BlockSpec((None, t, d), lat_blk),
                   pl.BlockSpec((None, t, d), lambda bi, j: (bi, j, 0))),
        compiler_params=_cparams(("parallel", "arbitrary")),
        name="rk_out",
    )(o_sum, proj, proj, proj, proj, a, x, ctx, mod0, mod0, mod1, mod1,
      norm_g1.reshape(1, d), k_a, r_k, ln_g, ln_b, w_out)


def _rt_proj_kernel(h_ref, w_ref, cos_ref, sin_ref, o_ref, *, dk):
    n = pl.program_id(0)
    tn = w_ref.shape[-1]
    half = dk // 2
    sub = min(ROW_TILE, h_ref.shape[0])
    chunks = [pl.ds(i, sub) for i in range(0, h_ref.shape[0], sub)]

    def project(rows):
        return jnp.dot(h_ref[rows, :], w_ref[...], preferred_element_type=F32)

    def rotate(scale):
        for rows in chunks:
            acc = project(rows)
            cos = cos_ref[rows, :]
            sin = sin_ref[rows, :]
            parts = []
            for h in range(tn // dk):
                x1 = acc[:, h * dk:h * dk + half]
                x2 = acc[:, h * dk + half:(h + 1) * dk]
                parts.append((x1 * cos - x2 * sin) * scale)
                parts.append((x1 * sin + x2 * cos) * scale)
            o_ref[rows, :] = jnp.concatenate(parts, axis=-1).astype(BF16)

    @pl.when(n == 0)
    def _():
        rotate(1.0)

    @pl.when(n == 1)
    def _():
        rotate(dk ** -0.5)

    @pl.when(n > 1)
    def _():
        for rows in chunks:
            o_ref[rows, :] = project(rows).astype(BF16)


def _rt_proj(h1, w_in, cos, sin, dk):
    m, d = h1.shape
    n_out = w_in.shape[1]
    tn = d
    tm = RT_PROJ_TILE if m % RT_PROJ_TILE == 0 else MM_TILE
    half = dk // 2
    return pl.pallas_call(
        functools.partial(_rt_proj_kernel, dk=dk),
        out_shape=jax.ShapeDtypeStruct((m, n_out), BF16),
        grid=(n_out // tn, m // tm),
        in_specs=[
            pl.BlockSpec((tm, d), lambda n, i: (i, 0)),
            pl.BlockSpec((d, tn), lambda n, i: (0, n)),
            pl.BlockSpec((tm, half), lambda n, i: (i, 0)),
            pl.BlockSpec((tm, half), lambda n, i: (i, 0)),
        ],
        out_specs=pl.BlockSpec((tm, tn), lambda n, i: (i, n)),
        compiler_params=_cparams(("parallel", "parallel")),
        name="rt_proj",
    )(h1, w_in, cos, sin)


def _log_sigmoid(x):
    return jnp.minimum(x, 0.0) - jnp.log1p(jnp.exp(-jnp.abs(x)))


def _ret_bstate_kernel(lg_ref, k_ref, v_ref, rb_ref, r_ref, *, dk, dv):
    c = k_ref.shape[0]
    s = pl.program_id(1)

    @pl.when(s == 0)
    def _():
        r_ref[...] = jnp.zeros_like(r_ref)

    pos = lax.broadcasted_iota(jnp.int32, (c, 1), 0).astype(F32)
    for h in range(RET_HEADS):
        lgb = _log_sigmoid(lg_ref[1, h])[:, 0:1]
        k = k_ref[:, h * dk:(h + 1) * dk]
        v = v_ref[:, h * dv:(h + 1) * dv]
        r0 = r_ref[h]
        rb_ref[h] = r0.astype(BF16)
        kd = (k.astype(F32) * jnp.exp(pos * lgb)).astype(BF16)
        r_ref[h] = r0 * jnp.exp(float(c) * lgb) + _dot(kd, v, _TN)


def _ret_main_kernel(lg_ref, q_ref, k_ref, v_ref, rb_ref, gng_ref, y_ref, rf_ref, dm_ref, dvec_ref,
                     *, dk, dv, n_ctx_blocks):
    c = q_ref.shape[0]
    s = pl.program_id(1)

    @pl.when(s == 0)
    def _():
        rf_ref[...] = jnp.zeros_like(rf_ref)
        diff = (lax.broadcasted_iota(jnp.int32, (c, c), 0)
                - lax.broadcasted_iota(jnp.int32, (c, c), 1)).astype(F32)
        pos = lax.broadcasted_iota(jnp.int32, (c, LANE), 0).astype(F32)
        for h in range(RET_HEADS):
            lgf = _log_sigmoid(lg_ref[0, h])
            lgb = _log_sigmoid(lg_ref[1, h])
            dm_ref[h] = jnp.where(diff >= 0, jnp.exp(jnp.maximum(diff, 0.0) * lgf[:, 0:1]),
                                  jnp.exp(jnp.maximum(-diff, 0.0) * lgb[:, 0:1]))
            dvec_ref[h, 0] = jnp.exp((pos + 1.0) * lgf)
            dvec_ref[h, 1] = jnp.exp((float(c) - pos) * lgb)
            dvec_ref[h, 2] = jnp.exp((float(c) - 1.0 - pos) * lgf)

    def lanes(x, width):
        return x[:, :width] if width <= LANE else jnp.concatenate([x] * (width // LANE), axis=1)

    @pl.when(s >= n_ctx_blocks)
    def _():
        scores = []
        for h in range(RET_HEADS):
            q = q_ref[:, h * dk:(h + 1) * dk]
            k = k_ref[:, h * dk:(h + 1) * dk]
            scores.append(_dot(q, k, _NT))
        outs = []
        for h in range(RET_HEADS):
            q = q_ref[:, h * dk:(h + 1) * dk]
            sc = (scores[h] * dm_ref[h]).astype(BF16)
            lhs = jnp.concatenate([sc, q * lanes(dvec_ref[h, 0], dk).astype(BF16),
                                   q * lanes(dvec_ref[h, 1], dk).astype(BF16)], axis=1)
            rhs = jnp.concatenate([v_ref[:, h * dv:(h + 1) * dv], rf_ref[h].astype(BF16), rb_ref[h]],
                                  axis=0)
            outs.append(_dot(lhs, rhs, _NN))
        for h in range(RET_HEADS):
            o = outs[h]
            on = o * lax.rsqrt(jnp.mean(o * o, axis=-1, keepdims=True) + EPS)
            y_ref[:, h * dv:(h + 1) * dv] = (on * gng_ref[:, h * dv:(h + 1) * dv]).astype(BF16)

    for h in range(RET_HEADS):
        k = k_ref[:, h * dk:(h + 1) * dk]
        v = v_ref[:, h * dv:(h + 1) * dv]
        kd = (k.astype(F32) * lanes(dvec_ref[h, 2], dk)).astype(BF16)
        gamma_c = jnp.exp(float(c) * _log_sigmoid(lg_ref[0, h]))
        rf_ref[h] = rf_ref[h] * lanes(gamma_c, dv) + _dot(kd, v, _TN)


def _ret(qkvg, decay_logit, gn_g, n_ctx_blocks, dk, dv):
    b, lt, _ = qkvg.shape
    c = RET_CHUNK
    nblk = lt // c
    nh = RET_HEADS
    d = nh * dk
    vd = nh * dv
    assert (2 * d) % vd == 0
    voff = (2 * d) // vd
    lg = jnp.broadcast_to(decay_logit.astype(F32)[:, :, None, None], (2, nh, 1, LANE))
    lg_spec = pl.BlockSpec((2, nh, 1, LANE), lambda bi, s: (0, 0, 0, 0))

    def bwd_blk(s):
        return jnp.where(s < n_ctx_blocks, n_ctx_blocks - 1 - s, nblk - 1 - (s - n_ctx_blocks))

    r_b = pl.pallas_call(
        functools.partial(_ret_bstate_kernel, dk=dk, dv=dv),
        out_shape=jax.ShapeDtypeStruct((b, nblk, nh, dk, dv), BF16),
        grid=(b, nblk),
        in_specs=[
            lg_spec,
            pl.BlockSpec((None, c, d), lambda bi, s: (bi, bwd_blk(s), 1)),
            pl.BlockSpec((None, c, vd), lambda bi, s: (bi, bwd_blk(s), voff)),
        ],
        out_specs=pl.BlockSpec((None, None, nh, dk, dv), lambda bi, s: (bi, bwd_blk(s), 0, 0, 0)),
        scratch_shapes=[pltpu.VMEM((nh, dk, dv), F32)],
        compiler_params=_cparams(("parallel", "arbitrary")),
        name="ret_bstate",
    )(lg, qkvg, qkvg)

    lat = lambda bi, s: (bi, jnp.maximum(s - n_ctx_blocks, 0), 0)
    return pl.pallas_call(
        functools.partial(_ret_main_kernel, dk=dk, dv=dv, n_ctx_blocks=n_ctx_blocks),
        out_shape=jax.ShapeDtypeStruct((b, lt - n_ctx_blocks * c, vd), BF16),
        grid=(b, nblk),
        in_specs=[
            lg_spec,
            pl.BlockSpec((None, c, d), lambda bi, s: (bi, s, 0)),
            pl.BlockSpec((None, c, d), lambda bi, s: (bi, s, 1)),
            pl.BlockSpec((None, c, vd), lambda bi, s: (bi, s, voff)),
            pl.BlockSpec((None, None, nh, dk, dv), lambda bi, s: (bi, s, 0, 0, 0)),
            pl.BlockSpec((1, vd), lambda bi, s: (0, 0)),
        ],
        out_specs=pl.BlockSpec((None, c, vd), lat),
        scratch_shapes=[pltpu.VMEM((nh, dk, dv), F32), pltpu.VMEM((nh, c, c), F32),
                        pltpu.VMEM((nh, 3, c, LANE), F32)],
        compiler_params=_cparams(("parallel", "arbitrary")),
        name="ret_main",
    )(lg, qkvg, qkvg, qkvg, r_b, gn_g.reshape(1, vd))


def _rt_out_kernel(y_ref, g_ref, x_ref, mod_ref, fg_ref, w_ref, o_ref):
    gated = y_ref[...].astype(F32) * _silu(g_ref[...].astype(F32))
    new = x_ref[...] + mod_ref[2:3, :] * _mm(gated, w_ref[...])
    o_ref[...] = new * lax.rsqrt(jnp.mean(new * new, axis=-1, keepdims=True) + EPS) * fg_ref[...]


def _rt_out(y, qkvg, x1, mod1, final_g, w_out, n_ctx_tiles):
    b, l, d = x1.shape
    vd = w_out.shape[0]
    t = ROW_TILE
    goff = (qkvg.shape[-1] - vd) // vd
    return pl.pallas_call(
        _rt_out_kernel,
        out_shape=jax.ShapeDtypeStruct((b, l, d), F32),
        grid=(b, l // t),
        in_specs=[
            pl.BlockSpec((None, t, vd), lambda bi, j: (bi, j, 0)),
            pl.BlockSpec((None, t, vd), lambda bi, j: (bi, j + n_ctx_tiles, goff)),
            pl.BlockSpec((None, t, d), lambda bi, j: (bi, j, 0)),
            pl.BlockSpec((None, 3, d), lambda bi, j: (bi, 0, 0)),
            pl.BlockSpec((1, d), lambda bi, j: (0, 0)),
            pl.BlockSpec((vd, d), lambda bi, j: (0, 0)),
        ],
        out_specs=pl.BlockSpec((None, t, d), lambda bi, j: (bi, j, 0)),
        compiler_params=_cparams(("parallel", "parallel")),
        name="rt_out",
    )(y, qkvg, x1, mod1, final_g.reshape(1, d), w_out)


def _rope_tables(l, lc, dk):
    t = jnp.arange(l)
    row = (t // GRID_W).astype(F32)
    col = (t % GRID_W).astype(F32)
    nf = dk // 4
    inv = ROPE_BASE ** (-jnp.arange(nf, dtype=F32) / nf)
    ang = jnp.concatenate([row[:, None] * inv, col[:, None] * inv], axis=-1)
    cos = jnp.concatenate([jnp.ones((lc, dk // 2), F32), jnp.cos(ang)], axis=0)
    sin = jnp.concatenate([jnp.zeros((lc, dk // 2), F32), jnp.sin(ang)], axis=0)
    return cos, sin


def kernel(x, c, ctx, c_ctx, ada_w, ada_b, norm_g, rk_mix, rk_w_in, rk_w0, rk_w1, rk_w2, rk_a0, rk_a1,
           rk_a2, rk_k_k, rk_k_a, rk_r_k, rk_ln_g, rk_ln_b, rk_w_out, rt_w_in, rt_decay_logit, rt_gn_g,
           rt_w_out, final_g):
    b, l, d = x.shape
    lc = ctx.shape[1]
    lt = lc + l
    assert lc == ROW_TILE and l % ROW_TILE == 0 and ROW_TILE % GRID_W == 0
    assert d % min(WKV_LANES, d) == 0 and d % WKV_GROUP == 0 and (b * lt) % MM_TILE == 0
    assert ada_w.shape[0] == 2 and rk_w_in.shape[0] == 1 and rt_w_in.shape[0] == 1
    dk = d // RET_HEADS
    dv = rt_w_out.shape[1] // RET_HEADS

    n_mod = -(-(b + 1) // 8) * 8
    cv = jnp.concatenate([c, c_ctx[None], jnp.zeros((n_mod - b - 1, d), F32)], axis=0)
    mod = _adaln(cv, ada_w, ada_b).reshape(2, n_mod, 3, d)

    xm, lw, a = _rk_pro(x, ctx, mod[0], norm_g[0], rk_mix[0], rk_w1[0].astype(BF16),
                        (0.5 * rk_w2[0]).astype(BF16), rk_a1[0].astype(BF16),
                        (0.5 * rk_a2[0]).astype(BF16), 0.5 * rk_w0[0], 0.5 * rk_a0[0])
    proj = _rk_proj(xm.reshape(4, b * lt, d), rk_w_in[0].astype(BF16)).reshape(4, b, lt, d)
    k_k = rk_k_k[0].reshape(1, d)
    k_a = rk_k_a[0].reshape(1, d)
    n_ctx_blocks = lc // ROW_TILE
    o_f = _wkv(proj, lw, a, k_k, k_a, 0, n_ctx_blocks)
    o_sum = _wkv(proj, lw, a, k_k, k_a, 1, n_ctx_blocks, prev=o_f)
    x1, h1 = _rk_out(o_sum, proj, a, x, ctx, mod[0], mod[1], norm_g[1], k_a,
                     rk_r_k[0].reshape(1, d), rk_ln_g[0].reshape(1, d), rk_ln_b[0].reshape(1, d),
                     rk_w_out[0].astype(BF16))

    cos, sin = _rope_tables(l, lc, dk)
    qkvg = _rt_proj(h1.reshape(b * lt, d), rt_w_in[0].astype(BF16), jnp.tile(cos, (b, 1)),
                    jnp.tile(sin, (b, 1)), dk)
    y = _ret(qkvg.reshape(b, lt, -1), rt_decay_logit[0], rt_gn_g[0], lc // RET_CHUNK, dk, dv)
    return _rt_out(y, qkvg.reshape(b, lt, -1), x1, mod[1], final_g, rt_w_out[0].astype(BF16),
                   lc // ROW_TILE)
```
